```python
import jax, jax.numpy as jnp
from jax import lax
import numpy as np

D_MODEL = 2048
BATCH = 4
SEQ = 2048
DEPTH = 4

GRID_W = 64
CTX_LEN = 256
EPS = 1e-6
N_BRANCH = 3
W_BRANCH = D_MODEL
LRU_BLOCKS = 16
LRU_BS = W_BRANCH // LRU_BLOCKS
CONV_W = 4
CONV_PAD_L = 2
LRU_C = 8.0
ML_HEADS = 8
ML_HD = W_BRANCH // ML_HEADS
ML_CHUNK = 64
M_INIT = -1e30
ATT_HD = 128
ATT_HEADS = W_BRANCH // ATT_HD
ATT_KV = 4
GQA = ATT_HEADS // ATT_KV
W_KV = ATT_KV * ATT_HD
Q_BLOCK = 128
ROPE_THETA = 10000.0
IN_SIZES = (W_BRANCH, W_BRANCH,
            W_BRANCH, W_BRANCH, W_BRANCH, W_BRANCH, W_BRANCH, 4 * ML_HEADS,
            W_BRANCH, W_KV, W_KV, W_BRANCH,
            N_BRANCH * D_MODEL)
N_IN = sum(IN_SIZES)

kernel_name = "hybrid_rglru_mlstm_gqa_prefix_dit"


def rmsnorm(x, g):
    xf = x.astype(jnp.float32)
    y = xf * lax.rsqrt(jnp.mean(xf * xf, axis=-1, keepdims=True) + EPS)
    return y * g.astype(jnp.float32)


def split_proj(p):
    parts, off = [], 0
    for n in IN_SIZES:
        parts.append(p[..., off:off + n])
        off += n
    return parts


def conv_centred(x, w, b):
    T = x.shape[1]
    xp = jnp.pad(x.astype(jnp.float32), ((0, 0), (CONV_PAD_L, CONV_W - 1 - CONV_PAD_L), (0, 0)))
    out = b.astype(jnp.float32)
    for k in range(CONV_W):
        out = out + xp[:, k:k + T] * w[k]
    return out


def blockdiag(x, w, b):
    xb = x.reshape(x.shape[0], x.shape[1], LRU_BLOCKS, LRU_BS)
    return jnp.einsum('btnc,ncd->btnd', xb, w).reshape(x.shape) + b


def linear_scan(a, bx, h0, reverse):
    def comb(e1, e2):
        a1, b1 = e1
        a2, b2 = e2
        return a1 * a2, a2 * b1 + b2
    A, Bc = lax.associative_scan(comb, (a, bx), axis=1, reverse=reverse)
    return A * h0[:, None] + Bc


def rglru(xc, wr, br, wi, bi, lam, h0, reverse):
    r = jax.nn.sigmoid(blockdiag(xc, wr, br))
    i = jax.nn.sigmoid(blockdiag(xc, wi, bi))
    log_a = -LRU_C * r * jax.nn.softplus(-lam)
    a = jnp.exp(log_a)
    mult = jnp.sqrt(-jnp.expm1(2.0 * log_a))
    h = linear_scan(a, mult * (i * xc), h0, reverse)
    final = h[:, 0] if reverse else h[:, -1]
    return h, final


def mlstm_chunkwise(q, k, v, ig, lf, state, with_out):
    B, H, T, d = q.shape
    nc = T // ML_CHUNK

    def to_chunks(a):
        return jnp.moveaxis(a.reshape(B, H, nc, ML_CHUNK, *a.shape[3:]), 2, 0)

    causal = jnp.tril(jnp.ones((ML_CHUNK, ML_CHUNK), dtype=bool))

    def step(carry, inp):
        C, n, m = carry
        qc, kc, vc, ic, fc = inp
        b = jnp.cumsum(fc, axis=-1)
        out = None
        if with_out:
            logD = jnp.where(causal, b[..., :, None] - b[..., None, :] + ic[..., None, :], -jnp.inf)
            m_inter = b + m[..., None]
            m_t = jnp.maximum(jnp.max(logD, axis=-1), m_inter)
            Dm = jnp.exp(logD - m_t[..., None])
            w_inter = jnp.exp(m_inter - m_t)
            s = jnp.einsum('bhtd,bhsd->bhts', qc, kc) * Dm
            num = jnp.einsum('bhts,bhse->bhte', s, vc) + w_inter[..., None] * jnp.einsum('bhtd,bhde->bhte', qc, C)
            den = jnp.sum(s, axis=-1) + w_inter * jnp.einsum('bhtd,bhd->bht', qc, n)
            out = num / jnp.maximum(jnp.abs(den), jnp.exp(-m_t))[..., None]
        bL = b[..., -1]
        log_w = bL[..., None] - b + ic
        m_new = jnp.maximum(bL + m, jnp.max(log_w, axis=-1))
        wk = jnp.exp(log_w - m_new[..., None])
        decay = jnp.exp(bL + m - m_new)
        C_new = decay[..., None, None] * C + jnp.einsum('bhs,bhsd,bhse->bhde', wk, kc, vc)
        n_new = decay[..., None] * n + jnp.einsum('bhs,bhsd->bhd', wk, kc)
        return (C_new, n_new, m_new), out

    state, hs = lax.scan(step, state, tuple(to_chunks(a) for a in (q, k, v, ig, lf)))
    h = jnp.moveaxis(hs, 0, 2).reshape(B, H, T, d) if with_out else None
    return h, state


def axial_rope_tables(row, col):
    n_freq = ATT_HD // 4
    inv = 1.0 / (ROPE_THETA ** (jnp.arange(n_freq, dtype=jnp.float32) / n_freq))
    ang = jnp.concatenate([row.astype(jnp.float32)[:, None] * inv,
                           col.astype(jnp.float32)[:, None] * inv], axis=-1)
    return jnp.cos(ang), jnp.sin(ang)


def rope_2d(x, cos, sin):
    half = ATT_HD // 2
    x1, x2 = x[..., :half], x[..., half:]
    return jnp.concatenate([x1 * cos - x2 * sin, x2 * cos + x1 * sin], axis=-1)


def attend_latent(q, k, v, k_ctx, v_ctx):
    B, H, S, d = q.shape
    keys = jnp.concatenate([k, k_ctx], axis=2)
    vals = jnp.concatenate([v, v_ctx], axis=2)
    nb = S // Q_BLOCK
    qb = q.reshape(B, ATT_KV, GQA, nb, Q_BLOCK, d).transpose(3, 0, 1, 2, 4, 5)

    def block(qi):
        s = jnp.einsum('bkgqd,bktd->bkgqt', qi, keys).astype(jnp.float32) * (ATT_HD ** -0.5)
        p = jax.nn.softmax(s, axis=-1)
        return jnp.einsum('bkgqt,bktd->bkgqd', p.astype(vals.dtype), vals)

    o = lax.map(block, qb)
    return o.transpose(1, 0, 4, 2, 3, 5).reshape(B, S, H * d)


def attend_ctx(q, k, v):
    B, H, C, d = q.shape
    qg = q.reshape(B, ATT_KV, GQA, C, d)
    s = jnp.einsum('bkgqd,bktd->bkgqt', qg, k).astype(jnp.float32) * (ATT_HD ** -0.5)
    p = jax.nn.softmax(s, axis=-1)
    o = jnp.einsum('bkgqt,bktd->bkgqd', p.astype(v.dtype), v)
    return o.transpose(0, 3, 1, 2, 4).reshape(B, C, H * d)


def hybrid_mixer(hx, hc, need_ctx, w_in, lru_conv_w, lru_conv_b, lru_wr, lru_br, lru_wi, lru_bi,
                 lru_lam, ml_gate_b, ml_norm, q_norm, k_norm, w_br, w_out, cos, sin):
    B = hx.shape[0]
    px = split_proj(hx @ w_in)
    pc = split_proj(hc @ w_in)

    xl = conv_centred(px[0], lru_conv_w, lru_conv_b)
    xcl = conv_centred(pc[0], lru_conv_w, lru_conv_b)
    zero = jnp.zeros((B, W_BRANCH), jnp.float32)
    lru_x, lru_c = 0.0, 0.0
    for dr, rev in enumerate((False, True)):
        h_c, fin = rglru(xcl, lru_wr[dr], lru_br[dr], lru_wi[dr], lru_bi[dr], lru_lam[dr], zero, rev)
        h_x, _ = rglru(xl, lru_wr[dr], lru_br[dr], lru_wi[dr], lru_bi[dr], lru_lam[dr], fin, rev)
        lru_x = lru_x + h_x
        if need_ctx:
            lru_c = lru_c + h_c
    y_lru_x = lru_x * jax.nn.silu(px[1])

    def ml_heads(a):
        return a.reshape(B, a.shape[1], ML_HEADS, ML_HD).transpose(0, 2, 1, 3).astype(jnp.float32)

    def ml_gates(a):
        g = a.reshape(B, a.shape[1], 2, 2, ML_HEADS).astype(jnp.float32) + ml_gate_b
        return g.transpose(2, 3, 0, 4, 1)

    qmx, kmx, vmx = ml_heads(px[2]), ml_heads(px[3]) * (ML_HD ** -0.5), ml_heads(px[4])
    qmc, kmc, vmc = ml_heads(pc[2]), ml_heads(pc[3]) * (ML_HD ** -0.5), ml_heads(pc[4])
    gx, gc = ml_gates(px[7]), ml_gates(pc[7])
    ml_x, ml_c = 0.0, 0.0
    for dr in range(2):
        fl = (lambda a: jnp.flip(a, axis=2)) if dr == 1 else (lambda a: a)
        st0 = (jnp.zeros((B, ML_HEADS, ML_HD, ML_HD), jnp.float32),
               jnp.zeros((B, ML_HEADS, ML_HD), jnp.float32),
               jnp.full((B, ML_HEADS), M_INIT, jnp.float32))
        h_c, st = mlstm_chunkwise(fl(qmc), fl(kmc), fl(vmc), fl(gc[dr, 0]),
                                  fl(jax.nn.log_sigmoid(gc[dr, 1])), st0, need_ctx)
        h_x, _ = mlstm_chunkwise(fl(qmx), fl(kmx), fl(vmx), fl(gx[dr, 0]),
                                 fl(jax.nn.log_sigmoid(gx[dr, 1])), st, True)
        ml_x = ml_x + fl(h_x)
        if need_ctx:
            ml_c = ml_c + fl(h_c)

    def ml_out(h, o, z):
        T = h.shape[2]
        h = jax.nn.sigmoid(o.astype(jnp.float32)) * h.transpose(0, 2, 1, 3).reshape(B, T, W_BRANCH)
        h = rmsnorm(h.reshape(B, T, ML_HEADS, ML_HD), ml_norm.reshape(ML_HEADS, ML_HD)).reshape(B, T, W_BRANCH)
        return h * jax.nn.silu(z)

    y_ml_x = ml_out(ml_x, px[5], px[6])

    def att_heads(a, H):
        return a.reshape(B, a.shape[1], H, ATT_HD).transpose(0, 2, 1, 3)

    qax = rope_2d(rmsnorm(att_heads(px[8], ATT_HEADS), q_norm), cos, sin)
    kax = rope_2d(rmsnorm(att_heads(px[9], ATT_KV), k_norm), cos, sin)
    vax = att_heads(px[10], ATT_KV).astype(jnp.float32)
    kac = rmsnorm(att_heads(pc[9], ATT_KV), k_norm)
    vac = att_heads(pc[10], ATT_KV).astype(jnp.float32)
    y_att_x = attend_latent(qax, kax, vax, kac, vac) * jax.nn.silu(px[11])

    def merge(ys, gate_logits):
        Y = jnp.stack(ys, axis=2)
        proj = jnp.einsum('btnw,nwd->btnd', Y, w_br)
        g = jax.nn.sigmoid(gate_logits.reshape(B, gate_logits.shape[1], N_BRANCH, D_MODEL))
        return jnp.einsum('btd,de->bte', jnp.sum(g * proj, axis=2), w_out)

    yx = merge([y_lru_x, y_ml_x, y_att_x], px[12])
    yc = None
    if need_ctx:
        y_lru_c = lru_c * jax.nn.silu(pc[1])
        y_ml_c = ml_out(ml_c, pc[5], pc[6])
        qac = rmsnorm(att_heads(pc[8], ATT_HEADS), q_norm)
        y_att_c = attend_ctx(qac, kac, vac) * jax.nn.silu(pc[11])
        yc = merge([y_lru_c, y_ml_c, y_att_c], pc[12])
    return yx, yc


def setup_inputs(seed: int = 0) -> dict:
    key = jax.random.key(seed)
    ks = jax.random.split(key, 24)
    f32 = jnp.float32
    D = D_MODEL

    def nrm(k, shape, scale):
        return jax.random.normal(k, shape, f32) * scale

    u = jax.random.uniform(ks[15], (DEPTH, 2, W_BRANCH), f32, 0.9, 0.999)
    a0 = u ** (1.0 / LRU_C)
    ml_i_b = nrm(ks[16], (DEPTH, 2, ML_HEADS), 0.1)
    ml_f_b = jnp.linspace(3.0, 6.0, ML_HEADS, dtype=f32) + nrm(ks[17], (DEPTH, 2, ML_HEADS), 0.1)
    return {
        "x": nrm(ks[0], (BATCH, SEQ, D), 1.0),
        "c": nrm(ks[1], (BATCH, D), 1.0),
        "ctx": nrm(ks[2], (BATCH, CTX_LEN, D), 1.0),
        "c_ctx": nrm(ks[3], (D,), 1.0),
        "ada_w": nrm(ks[4], (DEPTH, D, 3 * D), 0.5 * D ** -0.5),
        "ada_b": nrm(ks[5], (DEPTH, 3 * D), 0.02),
        "norm_pre": 1.0 + nrm(ks[6], (DEPTH, D), 0.05),
        "norm_post": 1.0 + nrm(ks[7], (DEPTH, D), 0.05),
        "w_in": nrm(ks[8], (DEPTH, D, N_IN), D ** -0.5),
        "lru_conv_w": nrm(ks[9], (DEPTH, CONV_W, W_BRANCH), CONV_W ** -0.5),
        "lru_conv_b": nrm(ks[10], (DEPTH, W_BRANCH), 0.02),
        "lru_wr": nrm(ks[11], (DEPTH, 2, LRU_BLOCKS, LRU_BS, LRU_BS), LRU_BS ** -0.5),
        "lru_br": nrm(ks[12], (DEPTH, 2, W_BRANCH), 0.02),
        "lru_wi": nrm(ks[13], (DEPTH, 2, LRU_BLOCKS, LRU_BS, LRU_BS), LRU_BS ** -0.5),
        "lru_bi": nrm(ks[14], (DEPTH, 2, W_BRANCH), 0.02),
        "lru_lam": jnp.log(a0) - jnp.log1p(-a0),
        "ml_gate_b": jnp.stack([ml_i_b, ml_f_b], axis=2),
        "ml_norm": 1.0 + nrm(ks[18], (DEPTH, W_BRANCH), 0.05),
        "q_norm": 1.0 + nrm(ks[19], (DEPTH, ATT_HD), 0.05),
        "k_norm": 1.0 + nrm(ks[20], (DEPTH, ATT_HD), 0.05),
        "w_br": nrm(ks[21], (DEPTH, N_BRANCH, W_BRANCH, D), W_BRANCH ** -0.5),
        "w_out": nrm(ks[22], (DEPTH, D, D), D ** -0.5),
    }


def reference(x, c, ctx, c_ctx, ada_w, ada_b, norm_pre, norm_post, w_in, lru_conv_w, lru_conv_b,
              lru_wr, lru_br, lru_wi, lru_bi, lru_lam, ml_gate_b, ml_norm, q_norm, k_norm, w_br, w_out):
    S = x.shape[1]
    rows = S // GRID_W
    row = jnp.repeat(jnp.arange(rows, dtype=jnp.int32), GRID_W)
    col = jnp.tile(jnp.arange(GRID_W, dtype=jnp.int32), rows)
    cos, sin = axial_rope_tables(row, col)
    u = ctx
    for l in range(DEPTH):
        need_ctx = l < DEPTH - 1
        shift_x, scale_x, gate_x = jnp.split(jax.nn.silu(c) @ ada_w[l] + ada_b[l], 3, axis=-1)
        shift_c, scale_c, gate_c = jnp.split(jax.nn.silu(c_ctx) @ ada_w[l] + ada_b[l], 3, axis=-1)
        hx = rmsnorm(x, norm_pre[l]) * (1.0 + scale_x[:, None]) + shift_x[:, None]
        hc = rmsnorm(u, norm_pre[l]) * (1.0 + scale_c) + shift_c
        yx, yc = hybrid_mixer(hx.astype(x.dtype), hc.astype(x.dtype), need_ctx, w_in[l], lru_conv_w[l],
                              lru_conv_b[l], lru_wr[l], lru_br[l], lru_wi[l], lru_bi[l], lru_lam[l],
                              ml_gate_b[l], ml_norm[l], q_norm[l], k_norm[l], w_br[l], w_out[l], cos, sin)
        x = (x + gate_x[:, None] * rmsnorm(yx, norm_post[l])).astype(x.dtype)
        if need_ctx:
            u = (u + gate_c * rmsnorm(yc, norm_post[l])).astype(u.dtype)
    return x
```

```python
import functools

import jax
import jax.numpy as jnp
from jax import lax
from jax.experimental import pallas as pl
from jax.experimental.pallas import tpu as pltpu

F32 = jnp.float32
BF16 = jnp.bfloat16

EPS = 1e-6
N_BRANCH = 3
LRU_BLOCKS = 16
CONV_W = 4
CONV_PAD_L = 2
LRU_C = 8.0
ML_HEADS = 8
ML_L = 256
ML_HEADS_PER_STEP = 2
M_INIT = -1e30
ATT_HD = 128
ATT_KV = 4
GRID_W = 64
ROPE_THETA = 10000.0
LANE = 128
COL_ALIGN = 512
VMEM_LIMIT = 56 * 1024 * 1024


def _cparams(sem):
    return pltpu.CompilerParams(dimension_semantics=sem, vmem_limit_bytes=VMEM_LIMIT)


def _sigmoid(x):
    return 1.0 / (1.0 + jnp.exp(-x))


def _silu(x):
    return x * _sigmoid(x)


def _dot(a, b):
    return jnp.dot(a, b, preferred_element_type=F32)


def _dot_nt(a, b):
    return lax.dot_general(a, b, (((1,), (1,)), ((), ())), preferred_element_type=F32)


def _dot_tn(a, b):
    return lax.dot_general(a, b, (((0,), (0,)), ((), ())), preferred_element_type=F32)


def _pick_tile(n, cands):
    for c in cands:
        if n % c == 0:
            return c
    raise ValueError(f"no tile for {n} among {cands}")


def _adaln_kernel(c_ref, w_ref, b_ref, o_ref):
    c = c_ref[...]
    s = _silu(c).astype(BF16)
    o_ref[0] = _dot(s, w_ref[0].astype(BF16)) + b_ref[0]


def _adaln(cc, ada_w, ada_b):
    depth, d, n3 = ada_w.shape
    tn = _pick_tile(n3, (768, 512, 384, 256, 128))
    return pl.pallas_call(
        _adaln_kernel,
        grid=(depth, n3 // tn),
        in_specs=[
            pl.BlockSpec((8, d), lambda l, j: (0, 0)),
            pl.BlockSpec((1, d, tn), lambda l, j: (l, 0, j)),
            pl.BlockSpec((1, 1, tn), lambda l, j: (l, 0, j)),
        ],
        out_specs=pl.BlockSpec((1, 8, tn), lambda l, j: (l, 0, j)),
        out_shape=jax.ShapeDtypeStruct((depth, 8, n3), F32),
        compiler_params=_cparams(("parallel", "parallel")),
        name="adaln",
    )(cc, ada_w, ada_b.reshape(depth, 1, n3))


def _row_select(i, tm, ctx_len, b, batch, mod_ref, lo, hi):
    row = i * tm + lax.broadcasted_iota(jnp.int32, (tm, 1), 0)
    vx = mod_ref[pl.ds(b, 1), lo:hi]
    vc = mod_ref[batch:batch + 1, lo:hi]
    return jnp.where(row < ctx_len, vc, vx)


def _normmod_kernel(x_ref, mod_ref, g_ref, o_ref, *, ctx_len, batch, d):
    b, i = pl.program_id(0), pl.program_id(1)
    tm = x_ref.shape[1]
    x = x_ref[0]
    y = x * lax.rsqrt(jnp.mean(x * x, axis=-1, keepdims=True) + EPS) * g_ref[...]
    shift = _row_select(i, tm, ctx_len, b, batch, mod_ref, 0, d)
    scale = _row_select(i, tm, ctx_len, b, batch, mod_ref, d, 2 * d)
    o_ref[0] = (y * (1.0 + scale) + shift).astype(BF16)


def _normmod(xu, mod_l, g, ctx_len):
    batch, t, d = xu.shape
    tm = _pick_tile(t, (768, 384, 256, 128))
    return pl.pallas_call(
        functools.partial(_normmod_kernel, ctx_len=ctx_len, batch=batch, d=d),
        grid=(batch, t // tm),
        in_specs=[
            pl.BlockSpec((1, tm, d), lambda b, i: (b, i, 0)),
            pl.BlockSpec((8, 3 * d), lambda b, i: (0, 0)),
            pl.BlockSpec((1, d), lambda b, i: (0, 0)),
        ],
        out_specs=pl.BlockSpec((1, tm, d), lambda b, i: (b, i, 0)),
        out_shape=jax.ShapeDtypeStruct((batch, t, d), BF16),
        compiler_params=_cparams(("parallel", "parallel")),
        name="normmod",
    )(xu, mod_l, g.reshape(1, d))


def _mm_kernel(a_ref, w_ref, o_ref):
    o_ref[0] = _dot(a_ref[0], w_ref[...])


def _in_proj(h, w):
    batch, t, k = h.shape
    n = w.shape[1]
    tm = _pick_tile(t, (1152, 768, 384, 256, 128))
    tn = COL_ALIGN
    return pl.pallas_call(
        _mm_kernel,
        grid=(batch, t // tm, n // tn),
        in_specs=[
            pl.BlockSpec((1, tm, k), lambda b, i, j: (b, i, 0)),
            pl.BlockSpec((k, tn), lambda b, i, j: (0, j)),
        ],
        out_specs=pl.BlockSpec((1, tm, tn), lambda b, i, j: (b, i, j)),
        out_shape=jax.ShapeDtypeStruct((batch, t, n), F32),
        compiler_params=_cparams(("parallel", "parallel", "arbitrary")),
        name="in_proj",
    )(h, w)


def _lru_kernel(x_ref, z_ref, cw_ref, cb_ref, wr_ref, br_ref, wi_ref, bi_ref, lam_ref, o_ref,
                xc_scr, hs_scr, a_scr, b_scr, *, ctx_len, tc):
    batch, t, bs = x_ref.shape
    n_lat = (t - ctx_len) // tc
    n_ctx = ctx_len // tc

    trow = lax.broadcasted_iota(jnp.int32, (t, 1), 0)
    seg = jnp.where(trow < ctx_len, trow, trow - ctx_len)
    seg_len = jnp.where(trow < ctx_len, ctx_len, t - ctx_len)
    cw = cw_ref[...]
    for b in range(batch):
        x = x_ref[b]
        acc = cb_ref[...] + x * cw[CONV_PAD_L:CONV_PAD_L + 1]
        for k in range(CONV_W):
            off = k - CONV_PAD_L
            if off == 0:
                continue
            shifted = pltpu.roll(x, (-off) % t, axis=0)
            ok = (seg + off >= 0) & (seg + off < seg_len)
            acc = acc + jnp.where(ok, shifted, 0.0) * cw[k:k + 1]
        xc_scr[b] = acc

    def gates(dr, t0):
        lam = lam_ref[dr:dr + 1, :]
        sp = jnp.maximum(-lam, 0.0) + jnp.log1p(jnp.exp(-jnp.abs(lam)))
        for b in range(batch):
            xc = xc_scr[b, pl.ds(t0, tc), :]
            xb = xc.astype(BF16)
            r = _sigmoid(_dot(xb, wr_ref[dr, 0].astype(BF16)) + br_ref[dr:dr + 1, :])
            ig = _sigmoid(_dot(xb, wi_ref[dr, 0].astype(BF16)) + bi_ref[dr:dr + 1, :])
            log_a = -LRU_C * r * sp
            a_scr[dr, pl.ds(b * tc, tc), :] = jnp.exp(log_a)
            th = jnp.tanh(log_a)
            b_scr[dr, pl.ds(b * tc, tc), :] = jnp.sqrt(-2.0 * th / (1.0 - th)) * (ig * xc)

    def scan_chunk(dr, h, t0, first):
        def step(s, h):
            tt = s if dr == 0 else tc - 1 - s
            a = a_scr[dr, pl.ds(tt, batch, stride=tc), :]
            bx = b_scr[dr, pl.ds(tt, batch, stride=tc), :]
            h = a * h + bx
            b_scr[dr, pl.ds(tt, batch, stride=tc), :] = h
            return h
        h = lax.fori_loop(0, tc, step, h, unroll=8)
        for b in range(batch):
            hb = b_scr[dr, pl.ds(b * tc, tc), :]
            if first:
                hs_scr[b, pl.ds(t0, tc), :] = hb
            else:
                hs_scr[b, pl.ds(t0, tc), :] = hs_scr[b, pl.ds(t0, tc), :] + hb
        return h

    zero = jnp.zeros((batch, bs), F32)

    def fwd_body(j, h):
        t0 = pl.multiple_of(j * tc, tc)
        gates(0, t0)
        return scan_chunk(0, h, t0, True)
    lax.fori_loop(0, t // tc, fwd_body, zero)

    def rev_ctx(j, h):
        t0 = pl.multiple_of((n_ctx - 1 - j) * tc, tc)
        gates(1, t0)
        return scan_chunk(1, h, t0, False)
    h = lax.fori_loop(0, n_ctx, rev_ctx, zero)

    def rev_lat(j, h):
        t0 = pl.multiple_of(ctx_len + (n_lat - 1 - j) * tc, tc)
        gates(1, t0)
        return scan_chunk(1, h, t0, False)
    lax.fori_loop(0, n_lat, rev_lat, h)

    for b in range(batch):
        o_ref[b] = (hs_scr[b] * _silu(z_ref[b])).astype(BF16)


def _lru(p, off_x, off_z, conv_w, conv_b, wr, br, wi, bi, lam, ctx_len):
    batch, t, _ = p.shape
    w = conv_w.shape[1]
    bs = w // LRU_BLOCKS
    assert bs == LANE, "RG-LRU gate block must be one lane tile wide"
    tc = _pick_tile(ctx_len, (256, 128))
    assert (t - ctx_len) % tc == 0
    bx, bz = off_x // bs, off_z // bs
    vec = lambda rows: pl.BlockSpec((rows, bs), lambda n: (0, n))
    wspec = pl.BlockSpec((2, 1, bs, bs), lambda n: (0, n, 0, 0))
    return pl.pallas_call(
        functools.partial(_lru_kernel, ctx_len=ctx_len, tc=tc),
        grid=(LRU_BLOCKS,),
        in_specs=[
            pl.BlockSpec((batch, t, bs), lambda n: (0, 0, bx + n)),
            pl.BlockSpec((batch, t, bs), lambda n: (0, 0, bz + n)),
            vec(CONV_W), vec(1), wspec, vec(2), wspec, vec(2), vec(2),
        ],
        out_specs=pl.BlockSpec((batch, t, bs), lambda n: (0, 0, n)),
        out_shape=jax.ShapeDtypeStruct((batch, t, w), BF16),
        scratch_shapes=[
            pltpu.VMEM((batch, t, bs), F32),
            pltpu.VMEM((batch, t, bs), F32),
            pltpu.VMEM((2, batch * tc, bs), F32),
            pltpu.VMEM((2, batch * tc, bs), F32),
        ],
        compiler_params=_cparams(("parallel",)),
        name="rglru",
    )(p, p, conv_w, conv_b.reshape(1, w), wr, br, wi, bi, lam)


def _split3_dot(x, m01):
    x1 = x.astype(BF16)
    r1 = x - x1.astype(F32)
    x2 = r1.astype(BF16)
    x3 = (r1 - x2.astype(F32)).astype(BF16)
    return _dot(x1, m01) + _dot(x2, m01) + _dot(x3, m01)


def _mlstm_kernel(bias_ref, q_ref, k_ref, v_ref, ig_ref, fg_ref, o_ref, a_scr, b_scr, c_scr,
                  *, ctx_len, hd):
    dr = pl.program_id(1)
    hp = pl.program_id(2)
    t = q_ref.shape[1]
    nck = t // ML_L
    n_ctx = ctx_len // ML_L
    sign = 1 - 2 * dr

    row = lax.broadcasted_iota(jnp.int32, (ML_L, ML_L), 0)
    col = lax.broadcasted_iota(jnp.int32, (ML_L, ML_L), 1)
    keep = (col - row) * sign <= 0
    eye = row == col
    cum01 = jnp.where((row - col) * sign <= 0, 1.0, 0.0).astype(BF16)

    for hh in range(ML_HEADS_PER_STEP):
        head = hp * ML_HEADS_PER_STEP + hh
        ig = ig_ref[0, hh] + bias_ref[dr * 2 * ML_HEADS + head]
        fpre = fg_ref[0, hh] + bias_ref[dr * 2 * ML_HEADS + ML_HEADS + head]
        lf = jnp.minimum(fpre, 0.0) - jnp.log1p(jnp.exp(-jnp.abs(fpre)))
        bcum = _split3_dot(lf, cum01)
        a_scr[hh] = ig - bcum
        b_scr[hh] = bcum
        c_scr[hh] = jnp.zeros((hd, hd), F32)

    def chunk(i, carry):
        ci = jnp.where(dr == 0, i, jnp.where(i < n_ctx, n_ctx - 1 - i, nck - 1 - (i - n_ctx)))
        r0 = pl.multiple_of(ci * ML_L, ML_L)
        new = []
        for hh in range(ML_HEADS_PER_STEP):
            m, n = carry[hh]
            a_row = a_scr[hh, pl.ds(ci, 1), :]
            b_row = b_scr[hh, pl.ds(ci, 1), :]
            q = q_ref[0, pl.ds(r0, ML_L), hh * hd:(hh + 1) * hd]
            k = k_ref[0, pl.ds(r0, ML_L), hh * hd:(hh + 1) * hd] * (hd ** -0.5)
            v = v_ref[0, pl.ds(r0, ML_L), hh * hd:(hh + 1) * hd].astype(BF16)
            qb = q.astype(BF16)

            b_col = jnp.sum(jnp.where(eye, b_row, 0.0), axis=1, keepdims=True)
            a_vis = jnp.where(keep, a_row, -jnp.inf)
            mx = jnp.maximum(jnp.max(a_vis, axis=1, keepdims=True), m)
            dm = jnp.exp(a_vis - mx)
            w_inter = jnp.exp(m - mx)
            s = _dot_nt(qb, k.astype(BF16)) * dm
            cmat = c_scr[hh]
            num = _dot(s.astype(BF16), v) + w_inter * _dot(qb, cmat.astype(BF16))
            den = jnp.sum(s, axis=1, keepdims=True) + w_inter * jnp.sum(q * n, axis=1, keepdims=True)
            out = num / jnp.maximum(jnp.abs(den), jnp.exp(-(b_col + mx)))
            o_ref[0, 0, pl.ds(r0, ML_L), hh * hd:(hh + 1) * hd] = out

            b_last = jnp.where(dr == 0, b_row[:, ML_L - 1:ML_L], b_row[:, 0:1])
            m_in = jnp.maximum(m, jnp.max(a_row, axis=1, keepdims=True))
            wk_row = jnp.exp(a_row - m_in)
            decay = jnp.exp(m - m_in)
            wk_col = jnp.sum(jnp.where(eye, wk_row, 0.0), axis=1, keepdims=True)
            kw = k * wk_col
            c_scr[hh] = decay * cmat + _dot_tn(kw.astype(BF16), v)
            n_new = decay * n + jnp.sum(kw, axis=0, keepdims=True)
            new.append((b_last + m_in, n_new))
        return tuple(new)

    init = tuple((jnp.full((1, 1), M_INIT, F32), jnp.zeros((1, hd), F32))
                 for _ in range(ML_HEADS_PER_STEP))
    lax.fori_loop(0, nck, chunk, init)


def _mlstm(p, off_q, off_k, off_v, gates_t, gate_b, ctx_len):
    batch, t, _ = p.shape
    w = off_k - off_q
    hd = w // ML_HEADS
    hps = ML_HEADS_PER_STEP
    cw = hps * hd
    nck = t // ML_L
    assert ctx_len % ML_L == 0 and t % ML_L == 0 and ML_HEADS % hps == 0
    g4 = gates_t.reshape(batch, 4 * ML_HEADS, nck, ML_L)
    qkv = lambda off: pl.BlockSpec((1, t, cw), lambda b, dr, hp, *_: (b, 0, off // cw + hp))
    gspec = lambda g: pl.BlockSpec(
        (1, hps, nck, ML_L),
        lambda b, dr, hp, *_: (b, (dr * 2 * ML_HEADS + g * ML_HEADS) // hps + hp, 0, 0))
    return pl.pallas_call(
        functools.partial(_mlstm_kernel, ctx_len=ctx_len, hd=hd),
        grid_spec=pltpu.PrefetchScalarGridSpec(
            num_scalar_prefetch=1,
            grid=(batch, 2, ML_HEADS // hps),
            in_specs=[qkv(off_q), qkv(off_k), qkv(off_v), gspec(0), gspec(1)],
            out_specs=pl.BlockSpec((1, 1, t, cw), lambda b, dr, hp, *_: (dr, b, 0, hp)),
            scratch_shapes=[
                pltpu.VMEM((hps, nck, ML_L), F32),
                pltpu.VMEM((hps, nck, ML_L), F32),
                pltpu.VMEM((hps, hd, hd), F32),
            ],
        ),
        out_shape=jax.ShapeDtypeStruct((2, batch, t, w), F32),
        compiler_params=_cparams(("parallel", "parallel", "parallel")),
        name="mlstm",
    )(gate_b.reshape(-1), p, p, p, g4, g4)


def _ml_out_kernel(h_ref, o_ref, z_ref, g_ref, y_ref):
    h = _sigmoid(o_ref[0]) * (h_ref[0, 0] + h_ref[1, 0])
    y = h * lax.rsqrt(jnp.mean(h * h, axis=-1, keepdims=True) + EPS) * g_ref[...]
    y_ref[0] = (y * _silu(z_ref[0])).astype(BF16)


def _ml_out(hdirs, p, off_o, off_z, ml_norm):
    _, batch, t, w = hdirs.shape
    hd = w // ML_HEADS
    tm = _pick_tile(t, (1152, 768, 384, 256, 128))
    col = lambda off: pl.BlockSpec((1, tm, hd), lambda b, i, h: (b, i, off // hd + h))
    return pl.pallas_call(
        _ml_out_kernel,
        grid=(batch, t // tm, ML_HEADS),
        in_specs=[
            pl.BlockSpec((2, 1, tm, hd), lambda b, i, h: (0, b, i, h)),
            col(off_o), col(off_z),
            pl.BlockSpec((1, hd), lambda b, i, h: (0, h)),
        ],
        out_specs=pl.BlockSpec((1, tm, hd), lambda b, i, h: (b, i, h)),
        out_shape=jax.ShapeDtypeStruct((batch, t, w), BF16),
        compiler_params=_cparams(("parallel", "parallel", "parallel")),
        name="ml_out",
    )(hdirs, p, p, ml_norm.reshape(1, w))


def _rms_rope(x, g, cos_f, sin_s):
    y = x * lax.rsqrt(jnp.mean(x * x, axis=-1, keepdims=True) + EPS) * g
    return y * cos_f + pltpu.roll(y, ATT_HD // 2, axis=1) * sin_s


def _attn_kernel(q_ref, k_ref, v_ref, z_ref, cq_ref, sq_ref, ck_ref, sk_ref, qn_ref, kn_ref, y_ref,
                 k_scr, v_scr, *, ctx_len, gqa):
    qi = pl.program_id(2)

    @pl.when(qi == 0)
    def _():
        k_scr[...] = _rms_rope(k_ref[0], kn_ref[...], ck_ref[...], sk_ref[...]).astype(BF16)
        v_scr[...] = v_ref[0].astype(BF16)

    def attend(nkeys):
        kb = k_scr[0:nkeys, :]
        vb = v_scr[0:nkeys, :]
        for g in range(gqa):
            sl = slice(g * ATT_HD, (g + 1) * ATT_HD)
            q = _rms_rope(q_ref[0, :, sl], qn_ref[...], cq_ref[...], sq_ref[...]) * (ATT_HD ** -0.5)
            s = _dot_nt(q.astype(BF16), kb)
            e = jnp.exp(s - jnp.max(s, axis=-1, keepdims=True))
            o = _dot(e.astype(BF16), vb) / jnp.sum(e, axis=-1, keepdims=True)
            y_ref[0, :, sl] = (o * _silu(z_ref[0, :, sl])).astype(BF16)

    @pl.when(qi == 0)
    def _():
        attend(ctx_len)

    @pl.when(qi > 0)
    def _():
        attend(k_scr.shape[0])


def _attention(p, off_q, off_k, off_v, off_z, cos_f, sin_s, q_norm, k_norm, ctx_len):
    batch, t, _ = p.shape
    w = off_z - off_q
    gqa = w // ATT_HD // ATT_KV
    gw = gqa * ATT_HD
    tq = ctx_len
    assert t % tq == 0
    qz = lambda off: pl.BlockSpec((1, tq, gw), lambda b, kv, i: (b, i, off // gw + kv))
    kvs = lambda off: pl.BlockSpec((1, t, ATT_HD), lambda b, kv, i: (b, 0, off // ATT_HD + kv))
    tab_q = pl.BlockSpec((tq, ATT_HD), lambda b, kv, i: (i, 0))
    tab_k = pl.BlockSpec((t, ATT_HD), lambda b, kv, i: (0, 0))
    nrm = pl.BlockSpec((1, ATT_HD), lambda b, kv, i: (0, 0))
    return pl.pallas_call(
        functools.partial(_attn_kernel, ctx_len=ctx_len, gqa=gqa),
        grid=(batch, ATT_KV, t // tq),
        in_specs=[qz(off_q), kvs(off_k), kvs(off_v), qz(off_z), tab_q, tab_q, tab_k, tab_k, nrm, nrm],
        out_specs=pl.BlockSpec((1, tq, gw), lambda b, kv, i: (b, i, kv)),
        out_shape=jax.ShapeDtypeStruct((batch, t, w), BF16),
        scratch_shapes=[pltpu.VMEM((t, ATT_HD), BF16), pltpu.VMEM((t, ATT_HD), BF16)],
        compiler_params=_cparams(("parallel", "parallel", "arbitrary")),
        name="attention",
    )(p, p, p, p, cos_f, sin_s, cos_f, sin_s, q_norm.reshape(1, ATT_HD), k_norm.reshape(1, ATT_HD))


def _merge_kernel(y0_ref, y1_ref, y2_ref, w_ref, g0_ref, g1_ref, g2_ref, o_ref):
    acc = None
    for n, (y_ref, g_ref) in enumerate(((y0_ref, g0_ref), (y1_ref, g1_ref), (y2_ref, g2_ref))):
        term = _sigmoid(g_ref[0]) * _dot(y_ref[0], w_ref[n])
        acc = term if acc is None else acc + term
    o_ref[0] = acc.astype(BF16)


def _merge(ys, w_br, p, off_g):
    batch, t, w = ys[0].shape
    d = w_br.shape[2]
    tm = _pick_tile(t, (768, 384, 256, 128))
    tn = _pick_tile(d, (512, 256, 128))
    yspec = pl.BlockSpec((1, tm, w), lambda b, i, j: (b, i, 0))
    gspec = lambda n: pl.BlockSpec((1, tm, tn), lambda b, i, j: (b, i, (off_g + n * d) // tn + j))
    return pl.pallas_call(
        _merge_kernel,
        grid=(batch, t // tm, d // tn),
        in_specs=[yspec, yspec, yspec,
                  pl.BlockSpec((N_BRANCH, w, tn), lambda b, i, j: (0, 0, j)),
                  gspec(0), gspec(1), gspec(2)],
        out_specs=pl.BlockSpec((1, tm, tn), lambda b, i, j: (b, i, j)),
        out_shape=jax.ShapeDtypeStruct((batch, t, d), BF16),
        compiler_params=_cparams(("parallel", "parallel", "arbitrary")),
        name="merge",
    )(*ys, w_br, p, p, p)


def _out_kernel(s_ref, w_ref, x_ref, mod_ref, g_ref, o_ref, *, ctx_len, batch, d):
    b, i = pl.program_id(0), pl.program_id(1)
    tm = s_ref.shape[1]
    y = _dot(s_ref[0], w_ref[...])
    yn = y * lax.rsqrt(jnp.mean(y * y, axis=-1, keepdims=True) + EPS) * g_ref[...]
    gate = _row_select(i, tm, ctx_len, b, batch, mod_ref, 2 * d, 3 * d)
    o_ref[0] = x_ref[0] + gate * yn


def _out_proj(zsum, w_out, xu, mod_l, g, ctx_len):
    batch, t, d = xu.shape
    tm = _pick_tile(t, (384, 256, 128))
    return pl.pallas_call(
        functools.partial(_out_kernel, ctx_len=ctx_len, batch=batch, d=d),
        grid=(batch, t // tm),
        in_specs=[
            pl.BlockSpec((1, tm, d), lambda b, i: (b, i, 0)),
            pl.BlockSpec((d, d), lambda b, i: (0, 0)),
            pl.BlockSpec((1, tm, d), lambda b, i: (b, i, 0)),
            pl.BlockSpec((8, 3 * d), lambda b, i: (0, 0)),
            pl.BlockSpec((1, d), lambda b, i: (0, 0)),
        ],
        out_specs=pl.BlockSpec((1, tm, d), lambda b, i: (b, i, 0)),
        out_shape=jax.ShapeDtypeStruct((batch, t, d), F32),
        compiler_params=_cparams(("parallel", "parallel")),
        name="out_proj",
    )(zsum, w_out, xu, mod_l, g.reshape(1, d))


def _pack_w_in(w_in_l, d, w_kv):
    w = d
    n_gate = 4 * ML_HEADS
    sizes = (w, w, w, w, w, w, w, n_gate, w, w_kv, w_kv, w, N_BRANCH * d)
    names = ("lru_x", "lru_z", "ml_q", "ml_k", "ml_v", "ml_o", "ml_z", "ml_g",
             "att_q", "att_k", "att_v", "att_z", "mg")
    src, o = {}, 0
    for nm, sz in zip(names, sizes):
        src[nm] = (o, sz)
        o += sz
    order = ("lru_x", "lru_z", "ml_q", "ml_k", "ml_v", "ml_o", "ml_z", "att_q", "att_z", "mg",
             "att_k", "att_v", "ml_g")
    cols, offs, o = [], {}, 0
    for nm in order:
        s0, sz = src[nm]
        offs[nm] = o
        cols.append(w_in_l[:, s0:s0 + sz])
        o += sz
    pad = (-o) % COL_ALIGN
    if pad:
        cols.append(jnp.zeros((w_in_l.shape[0], pad), w_in_l.dtype))
    return jnp.concatenate(cols, axis=1).astype(BF16), offs


def _rope_tables(seq, ctx_len):
    rows = seq // GRID_W
    row = jnp.repeat(jnp.arange(rows, dtype=jnp.int32), GRID_W).astype(F32)
    col = jnp.tile(jnp.arange(GRID_W, dtype=jnp.int32), rows).astype(F32)
    n_freq = ATT_HD // 4
    inv = 1.0 / (ROPE_THETA ** (jnp.arange(n_freq, dtype=F32) / n_freq))
    ang = jnp.concatenate([row[:, None] * inv, col[:, None] * inv], axis=-1)
    cos, sin = jnp.cos(ang), jnp.sin(ang)
    cos_f = jnp.concatenate([cos, cos], axis=-1)
    sin_s = jnp.concatenate([-sin, sin], axis=-1)
    cos_f = jnp.concatenate([jnp.ones((ctx_len, ATT_HD), F32), cos_f], axis=0)
    sin_s = jnp.concatenate([jnp.zeros((ctx_len, ATT_HD), F32), sin_s], axis=0)
    return cos_f, sin_s


def kernel(x, c, ctx, c_ctx, ada_w, ada_b, norm_pre, norm_post, w_in, lru_conv_w, lru_conv_b, lru_wr, lru_br, lru_wi, lru_bi, lru_lam, ml_gate_b, ml_norm, q_norm, k_norm, w_br, w_out):
    batch, seq, d = x.shape
    ctx_len = ctx.shape[1]
    depth = ada_w.shape[0]
    w_kv = ATT_KV * ATT_HD
    assert batch + 1 <= 8

    cos_f, sin_s = _rope_tables(seq, ctx_len)
    cc = jnp.concatenate([c, c_ctx[None], jnp.zeros((8 - batch - 1, d), F32)], axis=0)
    mod = _adaln(cc, ada_w, ada_b)
    xu = jnp.concatenate([ctx, x], axis=1)

    for l in range(depth):
        w_packed, off = _pack_w_in(w_in[l], d, w_kv)
        h = _normmod(xu, mod[l], norm_pre[l], ctx_len)
        p = _in_proj(h, w_packed)

        y_lru = _lru(p, off["lru_x"], off["lru_z"], lru_conv_w[l], lru_conv_b[l], lru_wr[l], lru_br[l],
                     lru_wi[l], lru_bi[l], lru_lam[l], ctx_len)

        g0 = off["ml_g"]
        gates_t = jnp.swapaxes(p[:, :, g0:g0 + 4 * ML_HEADS], 1, 2)
        hdirs = _mlstm(p, off["ml_q"], off["ml_k"], off["ml_v"], gates_t, ml_gate_b[l], ctx_len)
        y_ml = _ml_out(hdirs, p, off["ml_o"], off["ml_z"], ml_norm[l])

        y_att = _attention(p, off["att_q"], off["att_k"], off["att_v"], off["att_z"], cos_f, sin_s,
                           q_norm[l], k_norm[l], ctx_len)

        zsum = _merge((y_lru, y_ml, y_att), w_br[l].astype(BF16), p, off["mg"])
        xu = _out_proj(zsum, w_out[l].astype(BF16), xu, mod[l], norm_post[l], ctx_len)

    return xu[:, ctx_len:, :]
```

```python
import functools

import jax
import jax.numpy as jnp
from jax import lax
from jax.experimental import pallas as pl
from jax.experimental.pallas import tpu as pltpu

F32 = jnp.float32
BF16 = jnp.bfloat16

EPS = 1e-6
N_BRANCH = 3
LRU_BLOCKS = 16
CONV_W = 4
CONV_PAD_L = 2
LRU_C = 8.0
ML_HEADS = 8
ML_L = 256
M_INIT = -1e30
ATT_HD = 128
ATT_KV = 4
ATT_STACK = 1
GRID_W = 64
ROPE_THETA = 10000.0
LANE = 128
SUBLANE = 8
VMEM_LIMIT = 56 * 1024 * 1024


def _cparams(sem):
    return pltpu.CompilerParams(dimension_semantics=sem, vmem_limit_bytes=VMEM_LIMIT)


def _sigmoid(x):
    return 0.5 * jnp.tanh(0.5 * x) + 0.5


def _silu(x):
    return x * _sigmoid(x)


def _log_sigmoid(x):
    return jnp.minimum(x, 0.0) - jnp.log1p(jnp.exp(-jnp.abs(x)))


def _dot(a, b):
    return jnp.dot(a, b, preferred_element_type=F32)


def _dot_nt(a, b):
    return lax.dot_general(a, b, (((1,), (1,)), ((), ())), preferred_element_type=F32)


def _dot_tn(a, b):
    return lax.dot_general(a, b, (((0,), (0,)), ((), ())), preferred_element_type=F32)


def _split3(x):
    x1 = x.astype(BF16)
    r1 = x - x1.astype(F32)
    x2 = r1.astype(BF16)
    x3 = (r1 - x2.astype(F32)).astype(BF16)
    return x1, x2, x3


def _pick_tile(n, cands):
    for c in cands:
        if n % c == 0:
            return c
    raise ValueError(f"no tile for {n} among {cands}")


ADALN_SPLIT = 4


def _adaln_kernel(c_ref, *refs):
    w_refs, b_ref, o_ref = refs[:ADALN_SPLIT], refs[ADALN_SPLIT], refs[ADALN_SPLIT + 1]
    s = _silu(c_ref[...]).astype(BF16)
    tw = w_refs[0].shape[2]
    for q, w_ref in enumerate(w_refs):
        sl = slice(q * tw, (q + 1) * tw)
        o_ref[0, :, sl] = _dot(s, w_ref[0].astype(BF16)) + b_ref[0, :, sl]


def _adaln(cc, ada_w, ada_b):
    depth, d, n3 = ada_w.shape
    tn = _pick_tile(n3, (1536, 1024, 512))
    tw = tn // ADALN_SPLIT
    wspec = lambda q: pl.BlockSpec((1, d, tw), lambda l, j: (l, 0, j * ADALN_SPLIT + q))
    return pl.pallas_call(
        _adaln_kernel,
        grid=(depth, n3 // tn),
        in_specs=[pl.BlockSpec((8, d), lambda l, j: (0, 0))]
        + [wspec(q) for q in range(ADALN_SPLIT)]
        + [pl.BlockSpec((1, 1, tn), lambda l, j: (l, 0, j))],
        out_specs=pl.BlockSpec((1, 8, tn), lambda l, j: (l, 0, j)),
        out_shape=jax.ShapeDtypeStruct((depth, 8, n3), F32),
        compiler_params=_cparams(("parallel", "parallel")),
        name="adaln",
    )(cc, *([ada_w] * ADALN_SPLIT), ada_b.reshape(depth, 1, n3))


def _row_select(i, tm, ctx_len, b, batch, mod_ref, lo, hi):
    row = i * tm + lax.broadcasted_iota(jnp.int32, (tm, 1), 0)
    vx = mod_ref[pl.ds(b, 1), lo:hi]
    vc = mod_ref[batch:batch + 1, lo:hi]
    return jnp.where(row < ctx_len, vc, vx)


def _normmod_kernel(x_ref, mod_ref, g_ref, o_ref, *, ctx_len, batch, d):
    b, i = pl.program_id(0), pl.program_id(1)
    tm = x_ref.shape[1]
    x = x_ref[0]
    y = x * lax.rsqrt(jnp.mean(x * x, axis=-1, keepdims=True) + EPS) * g_ref[...]
    shift = _row_select(i, tm, ctx_len, b, batch, mod_ref, 0, d)
    scale = _row_select(i, tm, ctx_len, b, batch, mod_ref, d, 2 * d)
    o_ref[0] = (y * (1.0 + scale) + shift).astype(BF16)


def _normmod(xu, mod_l, g, ctx_len):
    batch, t, d = xu.shape
    tm = _pick_tile(t, (768, 384, 256, 128))
    return pl.pallas_call(
        functools.partial(_normmod_kernel, ctx_len=ctx_len, batch=batch, d=d),
        grid=(batch, t // tm),
        in_specs=[
            pl.BlockSpec((1, tm, d), lambda b, i: (b, i, 0)),
            pl.BlockSpec((8, 3 * d), lambda b, i: (0, 0)),
            pl.BlockSpec((1, d), lambda b, i: (0, 0)),
        ],
        out_specs=pl.BlockSpec((1, tm, d), lambda b, i: (b, i, 0)),
        out_shape=jax.ShapeDtypeStruct((batch, t, d), BF16),
        compiler_params=_cparams(("parallel", "parallel")),
        name="normmod",
    )(xu, mod_l, g.reshape(1, d))


def _in_proj_kernel(h_ref, w_ref, w2_ref, o_ref, w_scr, *, shift):
    @pl.when(pl.program_id(1) == 0)
    def _():
        k, tn = w_ref.shape
        rows = 256
        for r in range(0, k, rows):
            if shift == 0:
                w_scr[r:r + rows, :] = w_ref[r:r + rows, :].astype(BF16)
            else:
                cat = jnp.concatenate([w_ref[r:r + rows, :], w2_ref[r:r + rows, :]], axis=1)
                w_scr[r:r + rows, :] = pltpu.roll(cat, cat.shape[1] - shift, axis=1)[:, :tn].astype(BF16)

    o_ref[...] = _dot(h_ref[...], w_scr[...]).astype(o_ref.dtype)


def _in_proj(h2, w, c0, ncols, tn, shift, out_dtype):
    m, k = h2.shape
    tm = _pick_tile(m, (1152, 768, 512, 384, 256, 128))
    assert c0 % tn == 0 and ncols % tn == 0 and tn % LANE == 0 and k % 256 == 0
    return pl.pallas_call(
        functools.partial(_in_proj_kernel, shift=shift),
        grid=(ncols // tn, m // tm),
        in_specs=[
            pl.BlockSpec((tm, k), lambda j, i: (i, 0)),
            pl.BlockSpec((k, tn), lambda j, i: (0, c0 // tn + j)),
            pl.BlockSpec((k, LANE), lambda j, i: (0, (c0 + (j + 1) * tn) // LANE)),
        ],
        out_specs=pl.BlockSpec((tm, tn), lambda j, i: (i, j)),
        out_shape=jax.ShapeDtypeStruct((m, ncols), out_dtype),
        scratch_shapes=[pltpu.VMEM((k, tn), BF16)],
        compiler_params=_cparams(("parallel", "arbitrary")),
        name="in_proj",
    )(h2, w, w)


def _in_proj_t_kernel(h_ref, wt_ref, o_ref):
    o_ref[0] = _dot_nt(wt_ref[...], h_ref[0]).astype(o_ref.dtype)


def _in_proj_t(h, wt):
    batch, t, k = h.shape
    n = wt.shape[0]
    tm = _pick_tile(t, (768, 384, 256, 128))
    return pl.pallas_call(
        _in_proj_t_kernel,
        grid=(batch, t // tm),
        in_specs=[
            pl.BlockSpec((1, tm, k), lambda b, i: (b, i, 0)),
            pl.BlockSpec((n, k), lambda b, i: (0, 0)),
        ],
        out_specs=pl.BlockSpec((1, n, tm), lambda b, i: (b, 0, i)),
        out_shape=jax.ShapeDtypeStruct((batch, n, t), BF16),
        compiler_params=_cparams(("parallel", "parallel")),
        name="in_proj_t",
    )(h, wt)


def _lru_kernel(x_ref, z_ref, cw_ref, cb_ref, wr_ref, br_ref, wi_ref, bi_ref, lam_ref, o_ref,
                xc_scr, hs_scr, a_scr, b_scr, h_scr, *, ctx_len, tc, pitch):
    batch, t, bs = x_ref.shape
    nck = t // tc
    n_ctx = ctx_len // tc

    trow = lax.broadcasted_iota(jnp.int32, (t, 1), 0)
    seg = jnp.where(trow < ctx_len, trow, trow - ctx_len)
    seg_len = jnp.where(trow < ctx_len, ctx_len, t - ctx_len)
    cw = cw_ref[...]
    for b in range(batch):
        x = x_ref[b]
        acc = cb_ref[...] + x * cw[CONV_PAD_L:CONV_PAD_L + 1]
        for k in range(CONV_W):
            off = k - CONV_PAD_L
            if off == 0:
                continue
            shifted = pltpu.roll(x, (-off) % t, axis=0)
            ok = (seg + off >= 0) & (seg + off < seg_len)
            acc = acc + jnp.where(ok, shifted, 0.0) * cw[k:k + 1]
        xc_scr[b] = acc
        hs_scr[b] = jnp.zeros((t, bs), F32)

    wr = [(0.5 * wr_ref[dr, 0]).astype(BF16) for dr in range(2)]
    wi = [(0.5 * wi_ref[dr, 0]).astype(BF16) for dr in range(2)]
    br_h = 0.5 * br_ref[...]
    bi_h = 0.5 * bi_ref[...]
    lam = lam_ref[...]
    sp = jnp.maximum(-lam, 0.0) + jnp.log1p(jnp.exp(-jnp.abs(lam)))
    la_c = (-0.5 * LRU_C) * sp

    def gates(dr, t0):
        for b in range(batch):
            xc = xc_scr[b, pl.ds(t0, tc), :]
            xb = xc.astype(BF16)
            th_r = jnp.tanh(_dot(xb, wr[dr]) + br_h[dr:dr + 1, :])
            th_i = jnp.tanh(_dot(xb, wi[dr]) + bi_h[dr:dr + 1, :])
            log_a = la_c[dr:dr + 1, :] * th_r + la_c[dr:dr + 1, :]
            a_scr[dr, pl.ds(b * pitch, tc), :] = jnp.exp(log_a)
            u = jnp.abs(jnp.tanh(log_a))
            mult = lax.rsqrt(0.5 / u + 0.5)
            b_scr[dr, pl.ds(b * pitch, tc), :] = mult * ((th_i + 1.0) * (0.5 * xc))

    def two_steps(dr, h, t_a, t_b):
        at_a = pl.ds(t_a, batch, stride=pitch)
        at_b = pl.ds(t_b, batch, stride=pitch)
        a0, b0 = a_scr[dr, at_a, :], b_scr[dr, at_a, :]
        a1, b1 = a_scr[dr, at_b, :], b_scr[dr, at_b, :]
        h_a = a0 * h + b0
        h_b = (a1 * a0) * h + (a1 * b0 + b1)
        h_scr[dr, at_a, :] = h_a
        h_scr[dr, at_b, :] = h_b
        return h_b

    def chunk(i, carry):
        hf, hr = carry
        cr = jnp.where(i < n_ctx, n_ctx - 1 - i, nck - 1 - (i - n_ctx))
        tf0 = pl.multiple_of(i * tc, tc)
        tr0 = pl.multiple_of(cr * tc, tc)
        gates(0, tf0)
        gates(1, tr0)

        def pair(s, c):
            hf, hr = c
            hf = two_steps(0, hf, 2 * s, 2 * s + 1)
            hr = two_steps(1, hr, tc - 1 - 2 * s, tc - 2 - 2 * s)
            return hf, hr
        hf, hr = lax.fori_loop(0, tc // 2, pair, (hf, hr), unroll=4)
        for b in range(batch):
            hs_scr[b, pl.ds(tf0, tc), :] = hs_scr[b, pl.ds(tf0, tc), :] + h_scr[0, pl.ds(b * pitch, tc), :]
            hs_scr[b, pl.ds(tr0, tc), :] = hs_scr[b, pl.ds(tr0, tc), :] + h_scr[1, pl.ds(b * pitch, tc), :]
        return hf, hr

    zero = jnp.zeros((batch, bs), F32)
    lax.fori_loop(0, nck, chunk, (zero, zero))

    for b in range(batch):
        o_ref[b] = (hs_scr[b] * _silu(z_ref[b])).astype(BF16)


def _lru(p, off_x, off_z, conv_w, conv_b, wr, br, wi, bi, lam, ctx_len):
    batch, t, _ = p.shape
    w = conv_w.shape[1]
    bs = w // LRU_BLOCKS
    assert bs == LANE, "RG-LRU gate block must be one lane tile wide"
    tc = _pick_tile(ctx_len, (256, 128))
    assert (t - ctx_len) % tc == 0 and tc % 2 == 0
    pitch = tc + SUBLANE
    bx, bz = off_x // bs, off_z // bs
    vec = lambda rows: pl.BlockSpec((rows, bs), lambda n: (0, n))
    wspec = pl.BlockSpec((2, 1, bs, bs), lambda n: (0, n, 0, 0))
    return pl.pallas_call(
        functools.partial(_lru_kernel, ctx_len=ctx_len, tc=tc, pitch=pitch),
        grid=(LRU_BLOCKS,),
        in_specs=[
            pl.BlockSpec((batch, t, bs), lambda n: (0, 0, bx + n)),
            pl.BlockSpec((batch, t, bs), lambda n: (0, 0, bz + n)),
            vec(CONV_W), vec(1), wspec, vec(2), wspec, vec(2), vec(2),
        ],
        out_specs=pl.BlockSpec((batch, t, bs), lambda n: (0, 0, n)),
        out_shape=jax.ShapeDtypeStruct((batch, t, w), BF16),
        scratch_shapes=[
            pltpu.VMEM((batch, t, bs), F32),
            pltpu.VMEM((batch, t, bs), F32),
            pltpu.VMEM((2, batch * pitch, bs), F32),
            pltpu.VMEM((2, batch * pitch, bs), F32),
            pltpu.VMEM((2, batch * pitch, bs), F32),
        ],
        compiler_params=_cparams(("parallel",)),
        name="rglru",
    )(p, p, conv_w, conv_b.reshape(1, w), wr, br, wi, bi, lam)


def _mlstm_kernel(q_ref, kt_ref, v_ref, gc_ref, gr_ref, bc_ref, br_ref, o_ref, c_scr, m_scr, *, hd, nh):
    dr = pl.program_id(1)
    fwd = dr == 0
    sign = 1 - 2 * dr
    scale = hd ** -0.5

    @pl.when(pl.program_id(2) == 0)
    def _():
        c_scr[...] = jnp.zeros(c_scr.shape, F32)
        m_scr[...] = jnp.full(m_scr.shape, M_INIT, F32)

    row = lax.broadcasted_iota(jnp.int32, (ML_L, ML_L), 0)
    col = lax.broadcasted_iota(jnp.int32, (ML_L, ML_L), 1)
    keep = (col - row) * sign <= 0
    cum_c = jnp.where(keep, 1.0, 0.0).astype(BF16)
    cum_r = jnp.where((row - col) * sign <= 0, 1.0, 0.0).astype(BF16)

    g_c = gc_ref[0] + bc_ref[...]
    g_r = gr_ref[0, 0] + br_ref[...]
    x1, x2, x3 = _split3(_log_sigmoid(g_c))
    bcum_c = _dot(cum_c, x1) + _dot(cum_c, x2) + _dot(cum_c, x3)
    x1, x2, x3 = _split3(_log_sigmoid(g_r))
    bcum_r = _dot(x1, cum_r) + _dot(x2, cum_r) + _dot(x3, cum_r)

    def pick_c(arr, j):
        return jnp.where(fwd, arr[:, j:j + 1], arr[:, 2 * nh + j:2 * nh + j + 1])

    def pick_r(arr, j):
        return jnp.where(fwd, arr[j:j + 1, :], arr[2 * nh + j:2 * nh + j + 1, :])

    lane0 = lax.broadcasted_iota(jnp.int32, (ML_L, LANE), 1) == 0
    ones_col = jnp.where(lane0, 1.0, 0.0).astype(BF16)

    for h in range(nh):
        sl = slice(h * hd, (h + 1) * hd)
        b_col = pick_c(bcum_c, nh + h)
        b_row = pick_r(bcum_r, nh + h)
        a_row = pick_r(g_r, h) - b_row
        m = m_scr[h:h + 1, 0:1]
        q = q_ref[0, :, sl]
        kt = kt_ref[0, sl, :]
        v = v_ref[0, :, sl]
        v_aug = jnp.concatenate([v, ones_col], axis=1)

        a_vis = jnp.where(keep, a_row, -jnp.inf)
        mx = jnp.maximum(jnp.max(a_vis, axis=1, keepdims=True), m)
        mx_b = jnp.broadcast_to(mx, (ML_L, LANE))
        dm = jnp.exp(a_vis - jnp.concatenate([mx_b] * (ML_L // LANE), axis=1))
        w_inter_b = jnp.exp(m - mx_b)
        s = _dot(q, kt) * (dm * scale)
        c_aug = c_scr[h]
        qc = _dot(q, c_aug.astype(BF16))
        num = _dot(s.astype(BF16), v) + jnp.concatenate([w_inter_b] * (hd // LANE), axis=1) * qc[:, :hd]
        den = jnp.sum(s, axis=1, keepdims=True) + w_inter_b[:, 0:1] * qc[:, hd:hd + 1]
        o_ref[0, 0, :, sl] = num / jnp.maximum(jnp.abs(den), jnp.exp(-(b_col + mx)))

        b_last = jnp.where(fwd, b_row[:, ML_L - 1:ML_L], b_row[:, 0:1])
        m_in = jnp.maximum(m, jnp.max(a_row, axis=1, keepdims=True))
        decay = jnp.exp(m - m_in)
        kw_t = kt.astype(F32) * (jnp.exp(a_row - m_in) * scale)
        c_scr[h] = decay * c_aug + _dot(kw_t.astype(BF16), v_aug)
        m_scr[h:h + 1, :] = jnp.broadcast_to(b_last + m_in, (1, LANE))


def _mlstm(q, kt, v, pg, gate_b, ctx_len):
    batch, t, w = q.shape
    nh = ML_HEADS
    hd = w // nh
    nck = t // ML_L
    n_ctx = ctx_len // ML_L
    ng = 4 * nh
    assert ctx_len % ML_L == 0 and t % ML_L == 0 and ng <= LANE
    g_rows = jnp.swapaxes(pg[:, :, :ng].reshape(batch, nck, ML_L, ng), 2, 3)
    bias = gate_b.reshape(ng)
    bias_c = jnp.concatenate([bias, jnp.zeros((LANE - ng,), F32)]).reshape(1, LANE)
    bias_r = bias.reshape(ng, 1)

    def chunk_of(dr, i):
        rev = jnp.where(i < n_ctx, n_ctx - 1 - i, nck - 1 - (i - n_ctx))
        return jnp.where(dr == 0, i, rev)

    rows_spec = pl.BlockSpec((1, ML_L, w), lambda b, dr, i: (b, chunk_of(dr, i), 0))
    return pl.pallas_call(
        functools.partial(_mlstm_kernel, hd=hd, nh=nh),
        grid=(batch, 2, nck),
        in_specs=[
            rows_spec,
            pl.BlockSpec((1, w, ML_L), lambda b, dr, i: (b, 0, chunk_of(dr, i))),
            rows_spec,
            pl.BlockSpec((1, ML_L, LANE), lambda b, dr, i: (b, chunk_of(dr, i), 0)),
            pl.BlockSpec((1, 1, ng, ML_L), lambda b, dr, i: (b, chunk_of(dr, i), 0, 0)),
            pl.BlockSpec((1, LANE), lambda b, dr, i: (0, 0)),
            pl.BlockSpec((ng, 1), lambda b, dr, i: (0, 0)),
        ],
        out_specs=pl.BlockSpec((1, 1, ML_L, w), lambda b, dr, i: (dr, b, chunk_of(dr, i), 0)),
        out_shape=jax.ShapeDtypeStruct((2, batch, t, w), F32),
        scratch_shapes=[
            pltpu.VMEM((nh, hd, hd + LANE), F32),
            pltpu.VMEM((nh, LANE), F32),
        ],
        compiler_params=_cparams(("parallel", "parallel", "arbitrary")),
        name="mlstm",
    )(q, kt, v, pg, g_rows, bias_c, bias_r)


def _ml_out_kernel(h_ref, o_ref, z_ref, g_ref, y_ref):
    h = _sigmoid(o_ref[0]) * (h_ref[0, 0] + h_ref[1, 0])
    y = h * lax.rsqrt(jnp.mean(h * h, axis=-1, keepdims=True) + EPS) * g_ref[...]
    y_ref[0] = (y * _silu(z_ref[0])).astype(BF16)


def _ml_out(hdirs, p, off_o, off_z, ml_norm):
    _, batch, t, w = hdirs.shape
    hd = w // ML_HEADS
    tm = _pick_tile(t, (1152, 768, 384, 256, 128))
    col = lambda off: pl.BlockSpec((1, tm, hd), lambda b, i, h: (b, i, off // hd + h))
    return pl.pallas_call(
        _ml_out_kernel,
        grid=(batch, t // tm, ML_HEADS),
        in_specs=[
            pl.BlockSpec((2, 1, tm, hd), lambda b, i, h: (0, b, i, h)),
            col(off_o), col(off_z),
            pl.BlockSpec((1, hd), lambda b, i, h: (0, h)),
        ],
        out_specs=pl.BlockSpec((1, tm, hd), lambda b, i, h: (b, i, h)),
        out_shape=jax.ShapeDtypeStruct((batch, t, w), BF16),
        compiler_params=_cparams(("parallel", "parallel", "parallel")),
        name="ml_out",
    )(hdirs, p, p, ml_norm.reshape(1, w))


def _rms_rope(x, g, cos_f, sin_s):
    y = x * lax.rsqrt(jnp.mean(x * x, axis=-1, keepdims=True) + EPS) * g
    return y * cos_f + pltpu.roll(y, ATT_HD // 2, axis=1) * sin_s


def _attn_kernel(q_ref, k_ref, v_ref, z_ref, cq_ref, sq_ref, ck_ref, sk_ref, qn_ref, kn_ref, y_ref,
                 k_scr, v_scr, *, ctx_len, gqa):
    qi = pl.program_id(2)

    @pl.when(qi == 0)
    def _():
        k_scr[...] = _rms_rope(k_ref[0], kn_ref[...], ck_ref[...], sk_ref[...]).astype(BF16)
        v_scr[...] = v_ref[0].astype(BF16)

    def attend(nkeys):
        kb = k_scr[0:nkeys, :]
        vb = v_scr[0:nkeys, :]
        tq = q_ref.shape[1]
        for g0 in range(0, gqa, ATT_STACK):
            qs = [(_rms_rope(q_ref[0, :, g * ATT_HD:(g + 1) * ATT_HD], qn_ref[...], cq_ref[...], sq_ref[...])
                   * (ATT_HD ** -0.5)).astype(BF16) for g in range(g0, g0 + ATT_STACK)]
            s = _dot_nt(jnp.concatenate(qs, axis=0), kb)
            e = jnp.exp(s - jnp.max(s, axis=-1, keepdims=True))
            o = _dot(e.astype(BF16), vb) / jnp.sum(e, axis=-1, keepdims=True)
            for j in range(ATT_STACK):
                sl = slice((g0 + j) * ATT_HD, (g0 + j + 1) * ATT_HD)
                y_ref[0, :, sl] = (o[j * tq:(j + 1) * tq, :] * _silu(z_ref[0, :, sl])).astype(BF16)

    @pl.when(qi == 0)
    def _():
        attend(ctx_len)

    @pl.when(qi > 0)
    def _():
        attend(k_scr.shape[0])


def _attention(p, off_q, off_k, off_v, off_z, cos_f, sin_s, q_norm, k_norm, ctx_len):
    batch, t, _ = p.shape
    w = off_k - off_q
    gqa = w // ATT_HD // ATT_KV
    gw = gqa * ATT_HD
    tq = ctx_len
    assert t % tq == 0 and off_q % gw == 0 and off_z % gw == 0
    qz = lambda off: pl.BlockSpec((1, tq, gw), lambda b, kv, i: (b, i, off // gw + kv))
    kvs = lambda off: pl.BlockSpec((1, t, ATT_HD), lambda b, kv, i: (b, 0, off // ATT_HD + kv))
    tab_q = pl.BlockSpec((tq, ATT_HD), lambda b, kv, i: (i, 0))
    tab_k = pl.BlockSpec((t, ATT_HD), lambda b, kv, i: (0, 0))
    nrm = pl.BlockSpec((1, ATT_HD), lambda b, kv, i: (0, 0))
    return pl.pallas_call(
        functools.partial(_attn_kernel, ctx_len=ctx_len, gqa=gqa),
        grid=(batch, ATT_KV, t // tq),
        in_specs=[qz(off_q), kvs(off_k), kvs(off_v), qz(off_z), tab_q, tab_q, tab_k, tab_k, nrm, nrm],
        out_specs=pl.BlockSpec((1, tq, gw), lambda b, kv, i: (b, i, kv)),
        out_shape=jax.ShapeDtypeStruct((batch, t, w), BF16),
        scratch_shapes=[pltpu.VMEM((t, ATT_HD), BF16), pltpu.VMEM((t, ATT_HD), BF16)],
        compiler_params=_cparams(("parallel", "parallel", "arbitrary")),
        name="attention",
    )(p, p, p, p, cos_f, sin_s, cos_f, sin_s, q_norm.reshape(1, ATT_HD), k_norm.reshape(1, ATT_HD))


def _merge_kernel(y0_ref, y1_ref, y2_ref, w_ref, g0_ref, g1_ref, g2_ref, o_ref):
    acc = None
    for n, (y_ref, g_ref) in enumerate(((y0_ref, g0_ref), (y1_ref, g1_ref), (y2_ref, g2_ref))):
        term = _sigmoid(g_ref[0]) * _dot(y_ref[0], w_ref[n])
        acc = term if acc is None else acc + term
    o_ref[0] = acc.astype(BF16)


def _merge(ys, w_br, p, off_g):
    batch, t, w = ys[0].shape
    d = w_br.shape[2]
    tm = _pick_tile(t, (1152, 768, 384, 256, 128))
    tn = _pick_tile(d, (256, 128))
    assert off_g % tn == 0
    yspec = pl.BlockSpec((1, tm, w), lambda b, i, j: (b, i, 0))
    gspec = lambda n: pl.BlockSpec((1, tm, tn), lambda b, i, j: (b, i, (off_g + n * d) // tn + j))
    return pl.pallas_call(
        _merge_kernel,
        grid=(batch, t // tm, d // tn),
        in_specs=[yspec, yspec, yspec,
                  pl.BlockSpec((N_BRANCH, w, tn), lambda b, i, j: (0, 0, j)),
                  gspec(0), gspec(1), gspec(2)],
        out_specs=pl.BlockSpec((1, tm, tn), lambda b, i, j: (b, i, j)),
        out_shape=jax.ShapeDtypeStruct((batch, t, d), BF16),
        compiler_params=_cparams(("parallel", "parallel", "arbitrary")),
        name="merge",
    )(*ys, w_br, p, p, p)


def _out_kernel(s_ref, w_ref, x_ref, mod_ref, g_ref, o_ref, *, ctx_len, batch, d):
    b, i = pl.program_id(0), pl.program_id(1)
    tm = s_ref.shape[1]
    y = _dot(s_ref[0], w_ref[...])
    yn = y * lax.rsqrt(jnp.mean(y * y, axis=-1, keepdims=True) + EPS) * g_ref[...]
    gate = _row_select(i, tm, ctx_len, b, batch, mod_ref, 2 * d, 3 * d)
    o_ref[0] = x_ref[0] + gate * yn


def _out_proj(zsum, w_out, xu, mod_l, g, ctx_len):
    batch, t, d = xu.shape
    tm = _pick_tile(t, (768, 384, 256, 128))
    return pl.pallas_call(
        functools.partial(_out_kernel, ctx_len=ctx_len, batch=batch, d=d),
        grid=(batch, t // tm),
        in_specs=[
            pl.BlockSpec((1, tm, d), lambda b, i: (b, i, 0)),
            pl.BlockSpec((d, d), lambda b, i: (0, 0)),
            pl.BlockSpec((1, tm, d), lambda b, i: (b, i, 0)),
            pl.BlockSpec((8, 3 * d), lambda b, i: (0, 0)),
            pl.BlockSpec((1, d), lambda b, i: (0, 0)),
        ],
        out_specs=pl.BlockSpec((1, tm, d), lambda b, i: (b, i, 0)),
        out_shape=jax.ShapeDtypeStruct((batch, t, d), F32),
        compiler_params=_cparams(("parallel", "parallel")),
        name="out_proj",
    )(zsum, w_out, xu, mod_l, g.reshape(1, d))


def _rope_tables(seq, ctx_len):
    rows = seq // GRID_W
    row = jnp.repeat(jnp.arange(rows, dtype=jnp.int32), GRID_W).astype(F32)
    col = jnp.tile(jnp.arange(GRID_W, dtype=jnp.int32), rows).astype(F32)
    n_freq = ATT_HD // 4
    inv = 1.0 / (ROPE_THETA ** (jnp.arange(n_freq, dtype=F32) / n_freq))
    ang = jnp.concatenate([row[:, None] * inv, col[:, None] * inv], axis=-1)
    cos, sin = jnp.cos(ang), jnp.sin(ang)
    cos_f = jnp.concatenate([cos, cos], axis=-1)
    sin_s = jnp.concatenate([-sin, sin], axis=-1)
    cos_f = jnp.concatenate([jnp.ones((ctx_len, ATT_HD), F32), cos_f], axis=0)
    sin_s = jnp.concatenate([jnp.zeros((ctx_len, ATT_HD), F32), sin_s], axis=0)
    return cos_f, sin_s


def kernel(x, c, ctx, c_ctx, ada_w, ada_b, norm_pre, norm_post, w_in, lru_conv_w, lru_conv_b, lru_wr, lru_br, lru_wi, lru_bi, lru_lam, ml_gate_b, ml_norm, q_norm, k_norm, w_br, w_out):
    batch, seq, d = x.shape
    ctx_len = ctx.shape[1]
    t = ctx_len + seq
    depth = ada_w.shape[0]
    w = d
    w_kv = ATT_KV * ATT_HD
    n_gate = 4 * ML_HEADS
    assert batch + 1 <= 8

    c_lru, c_qkv, c_oz, c_gate = 0, 2 * w, 5 * w, 7 * w
    n_att = 2 * w + 2 * w_kv + N_BRANCH * d
    tn = _pick_tile(n_att, (1024, 512, 256))
    assert w % tn == 0
    off_aq, off_ak, off_av, off_az, off_mg = 0, w, w + w_kv, w + 2 * w_kv, 2 * w + 2 * w_kv

    cos_f, sin_s = _rope_tables(seq, ctx_len)
    cc = jnp.concatenate([c, c_ctx[None], jnp.zeros((8 - batch - 1, d), F32)], axis=0)
    mod = _adaln(cc, ada_w, ada_b)
    xu = jnp.concatenate([ctx, x], axis=1)

    for l in range(depth):
        h3 = _normmod(xu, mod[l], norm_pre[l], ctx_len)
        h2 = h3.reshape(batch * t, d)
        proj = lambda c0, n, tile, shift, dt: _in_proj(h2, w_in[l], c0, n, tile, shift, dt).reshape(batch, t, n)
        p_lru = proj(c_lru, 2 * w, tn, 0, F32)
        p_q = proj(c_qkv, w, tn, 0, BF16)
        p_kt = _in_proj_t(h3, w_in[l][:, c_qkv + w:c_qkv + 2 * w].T.astype(BF16))
        p_v = proj(c_qkv + 2 * w, w, tn, 0, BF16)
        p_oz = proj(c_oz, 2 * w, tn, 0, F32)
        p_gate = proj(c_gate, LANE, LANE, 0, F32)
        p_att = proj(c_gate, n_att, tn, n_gate, F32)

        y_lru = _lru(p_lru, 0, w, lru_conv_w[l], lru_conv_b[l], lru_wr[l], lru_br[l],
                     lru_wi[l], lru_bi[l], lru_lam[l], ctx_len)
        hdirs = _mlstm(p_q, p_kt, p_v, p_gate, ml_gate_b[l], ctx_len)
        y_ml = _ml_out(hdirs, p_oz, 0, w, ml_norm[l])
        y_att = _attention(p_att, off_aq, off_ak, off_av, off_az, cos_f, sin_s, q_norm[l], k_norm[l], ctx_len)

        zsum = _merge((y_lru, y_ml, y_att), w_br[l].astype(BF16), p_att, off_mg)
        xu = _out_proj(zsum, w_out[l].astype(BF16), xu, mod[l], norm_post[l], ctx_len)

    return xu[:, ctx_len:, :]
```

```python
import functools
import math

import jax
import jax.numpy as jnp
from jax import lax
from jax.experimental import pallas as pl
from jax.experimental.pallas import tpu as pltpu

F32 = jnp.float32
BF16 = jnp.bfloat16

EPS = 1e-6
N_BRANCH = 3
LRU_BLOCKS = 16
CONV_W = 4
CONV_PAD_L = 2
LRU_C = 8.0
ML_HEADS = 8
ML_L = 256
M_INIT = -1e30
ATT_HD = 128
ATT_KV = 4
ATT_STACK = 1
GRID_W = 64
ROPE_THETA = 10000.0
LOG2_E = 1.4426950408889634
LANE = 128
SUBLANE = 8
VMEM_LIMIT = 56 * 1024 * 1024


def _cparams(sem):
    return pltpu.CompilerParams(dimension_semantics=sem, vmem_limit_bytes=VMEM_LIMIT)


def _sigmoid(x):
    return 0.5 * jnp.tanh(0.5 * x) + 0.5


def _silu(x):
    return x * _sigmoid(x)


def _log_sigmoid(x):
    return jnp.minimum(x, 0.0) - jnp.log1p(jnp.exp(-jnp.abs(x)))


def _dot(a, b):
    return jnp.dot(a, b, preferred_element_type=F32)


def _dot_nt(a, b):
    return lax.dot_general(a, b, (((1,), (1,)), ((), ())), preferred_element_type=F32)


def _dot_tn(a, b):
    return lax.dot_general(a, b, (((0,), (0,)), ((), ())), preferred_element_type=F32)


def _split3(x):
    x1 = x.astype(BF16)
    r1 = x - x1.astype(F32)
    x2 = r1.astype(BF16)
    x3 = (r1 - x2.astype(F32)).astype(BF16)
    return x1, x2, x3


def _pick_tile(n, cands):
    for c in cands:
        if n % c == 0:
            return c
    raise ValueError(f"no tile for {n} among {cands}")


def _adaln_kernel(c_ref, w_ref, b_ref, o_ref):
    @pl.when(pl.program_id(1) == 0)
    def _():
        o_ref[0] = jnp.broadcast_to(b_ref[0], o_ref.shape[1:])

    s = _silu(c_ref[...]).astype(BF16)
    o_ref[0] += _dot(s, w_ref[0].astype(BF16))


def _adaln(cc, ada_w, ada_b):
    depth, d, n3 = ada_w.shape
    tk = _pick_tile(d, (256, 128))
    return pl.pallas_call(
        _adaln_kernel,
        grid=(depth, d // tk),
        in_specs=[
            pl.BlockSpec((8, tk), lambda l, k: (0, k)),
            pl.BlockSpec((1, tk, n3), lambda l, k: (l, k, 0)),
            pl.BlockSpec((1, 1, n3), lambda l, k: (l, 0, 0)),
        ],
        out_specs=pl.BlockSpec((1, 8, n3), lambda l, k: (l, 0, 0)),
        out_shape=jax.ShapeDtypeStruct((depth, 8, n3), F32),
        compiler_params=_cparams(("parallel", "arbitrary")),
        name="adaln",
    )(cc, ada_w, ada_b.reshape(depth, 1, n3))


def _row_select(i, tm, ctx_len, b, batch, mod_ref, lo, hi):
    row = i * tm + lax.broadcasted_iota(jnp.int32, (tm, 1), 0)
    vx = mod_ref[pl.ds(b, 1), lo:hi]
    vc = mod_ref[batch:batch + 1, lo:hi]
    return jnp.where(row < ctx_len, vc, vx)


def _normmod_kernel(x_ref, mod_ref, g_ref, o_ref, *, ctx_len, batch, d):
    b, i = pl.program_id(0), pl.program_id(1)
    tm = x_ref.shape[1]
    x = x_ref[0]
    y = x * lax.rsqrt(jnp.mean(x * x, axis=-1, keepdims=True) + EPS) * g_ref[...]
    shift = _row_select(i, tm, ctx_len, b, batch, mod_ref, 0, d)
    scale = _row_select(i, tm, ctx_len, b, batch, mod_ref, d, 2 * d)
    o_ref[0] = (y * (1.0 + scale) + shift).astype(BF16)


def _normmod(xu, mod_l, g, ctx_len):
    batch, t, d = xu.shape
    tm = _pick_tile(t, (768, 384, 256, 128))
    return pl.pallas_call(
        functools.partial(_normmod_kernel, ctx_len=ctx_len, batch=batch, d=d),
        grid=(batch, t // tm),
        in_specs=[
            pl.BlockSpec((1, tm, d), lambda b, i: (b, i, 0)),
            pl.BlockSpec((8, 3 * d), lambda b, i: (0, 0)),
            pl.BlockSpec((1, d), lambda b, i: (0, 0)),
        ],
        out_specs=pl.BlockSpec((1, tm, d), lambda b, i: (b, i, 0)),
        out_shape=jax.ShapeDtypeStruct((batch, t, d), BF16),
        compiler_params=_cparams(("parallel", "parallel")),
        name="normmod",
    )(xu, mod_l, g.reshape(1, d))


def _in_proj_kernel(h_ref, w_ref, w2_ref, o_ref, w_scr, *, shift):
    @pl.when(pl.program_id(1) == 0)
    def _():
        _, k, tn = w_ref.shape
        rows = 256
        for r in range(0, k, rows):
            if shift == 0:
                w_scr[r:r + rows, :] = w_ref[0, r:r + rows, :].astype(BF16)
            else:
                cat = jnp.concatenate([w_ref[0, r:r + rows, :], w2_ref[0, r:r + rows, :]], axis=1)
                w_scr[r:r + rows, :] = pltpu.roll(cat, cat.shape[1] - shift, axis=1)[:, :tn].astype(BF16)

    o_ref[...] = _dot(h_ref[...], w_scr[...]).astype(o_ref.dtype)


def _in_proj(h2, w, l, c0, ncols, tn, shift, out_dtype):
    m, k = h2.shape
    tm = _pick_tile(m, (1152, 768, 512, 384, 256, 128))
    assert c0 % tn == 0 and ncols % tn == 0 and tn % LANE == 0 and k % 256 == 0
    return pl.pallas_call(
        functools.partial(_in_proj_kernel, shift=shift),
        grid=(ncols // tn, m // tm),
        in_specs=[
            pl.BlockSpec((tm, k), lambda j, i: (i, 0)),
            pl.BlockSpec((1, k, tn), lambda j, i: (l, 0, c0 // tn + j)),
            pl.BlockSpec((1, k, LANE), lambda j, i: (l, 0, (c0 + (j + 1) * tn) // LANE)),
        ],
        out_specs=pl.BlockSpec((tm, tn), lambda j, i: (i, j)),
        out_shape=jax.ShapeDtypeStruct((m, ncols), out_dtype),
        scratch_shapes=[pltpu.VMEM((k, tn), BF16)],
        compiler_params=_cparams(("parallel", "arbitrary")),
        name="in_proj",
    )(h2, w, w)


def _in_proj_t_kernel(h_ref, wt_ref, o_ref):
    o_ref[0] = _dot_nt(wt_ref[...], h_ref[0]).astype(o_ref.dtype)


def _in_proj_t(h, wt):
    batch, t, k = h.shape
    n = wt.shape[0]
    tm = _pick_tile(t, (768, 384, 256, 128))
    return pl.pallas_call(
        _in_proj_t_kernel,
        grid=(batch, t // tm),
        in_specs=[
            pl.BlockSpec((1, tm, k), lambda b, i: (b, i, 0)),
            pl.BlockSpec((n, k), lambda b, i: (0, 0)),
        ],
        out_specs=pl.BlockSpec((1, n, tm), lambda b, i: (b, 0, i)),
        out_shape=jax.ShapeDtypeStruct((batch, n, t), BF16),
        compiler_params=_cparams(("parallel", "parallel")),
        name="in_proj_t",
    )(h, wt)


def _lru_kernel(x_ref, z_ref, cw_ref, cb_ref, wr_ref, br_ref, wi_ref, bi_ref, lam_ref, o_ref,
                xc_scr, hs_scr, a_scr, b_scr, h_scr, cwm_scr, *, ctx_len, tc, pitch):
    batch, t, bs = x_ref.shape
    nck = t // tc
    n_ctx = ctx_len // tc

    trow = lax.broadcasted_iota(jnp.int32, (t, 1), 0)
    seg = jnp.where(trow < ctx_len, trow, trow - ctx_len)
    seg_len = jnp.where(trow < ctx_len, ctx_len, t - ctx_len)
    cw = cw_ref[...]
    taps = [k for k in range(CONV_W) if k != CONV_PAD_L]
    for k in taps:
        off = k - CONV_PAD_L
        ok = (seg + off >= 0) & (seg + off < seg_len)
        cwm_scr[k] = jnp.where(ok, cw[k:k + 1], 0.0)
    cw_c = cw[CONV_PAD_L:CONV_PAD_L + 1]
    edge = 2 * SUBLANE
    for b in range(batch):
        lo, hi = SUBLANE, t - SUBLANE
        acc = cb_ref[...] + x_ref[b, lo:hi, :] * cw_c
        for k in taps:
            off = k - CONV_PAD_L
            acc = acc + x_ref[b, lo + off:hi + off, :] * cwm_scr[k, lo:hi, :]
        xc_scr[b, lo:hi, :] = acc
        for s0, keep_lo in ((0, 0), (t - edge, SUBLANE)):
            e = x_ref[b, s0:s0 + edge, :]
            acc = cb_ref[...] + e * cw_c
            for k in taps:
                acc = acc + pltpu.roll(e, (CONV_PAD_L - k) % edge, axis=0) * cwm_scr[k, s0:s0 + edge, :]
            xc_scr[b, s0 + keep_lo:s0 + keep_lo + SUBLANE, :] = acc[keep_lo:keep_lo + SUBLANE]
        hs_scr[b] = jnp.zeros((t, bs), F32)

    wr = [(0.5 * wr_ref[dr, 0]).astype(BF16) for dr in range(2)]
    wi = [(0.5 * wi_ref[dr, 0]).astype(BF16) for dr in range(2)]
    br_h = 0.5 * br_ref[...]
    bi_h = 0.5 * bi_ref[...]
    lam = lam_ref[...]
    sp = jnp.maximum(-lam, 0.0) + jnp.log1p(jnp.exp(-jnp.abs(lam)))
    la_c = (-0.5 * LRU_C) * sp

    def gates(dr, t0):
        for b in range(batch):
            xc = xc_scr[b, pl.ds(t0, tc), :]
            xb = xc.astype(BF16)
            th_r = jnp.tanh(_dot(xb, wr[dr]) + br_h[dr:dr + 1, :])
            th_i = jnp.tanh(_dot(xb, wi[dr]) + bi_h[dr:dr + 1, :])
            log_a = la_c[dr:dr + 1, :] * th_r + la_c[dr:dr + 1, :]
            a_scr[dr, pl.ds(b * pitch, tc), :] = jnp.exp(log_a)
            u = jnp.abs(jnp.tanh(log_a))
            mult = lax.rsqrt(0.5 / u + 0.5)
            b_scr[dr, pl.ds(b * pitch, tc), :] = mult * ((th_i + 1.0) * (0.5 * xc))

    def two_steps(dr, h, t_a, t_b):
        at_a = pl.ds(t_a, batch, stride=pitch)
        at_b = pl.ds(t_b, batch, stride=pitch)
        a0, b0 = a_scr[dr, at_a, :], b_scr[dr, at_a, :]
        a1, b1 = a_scr[dr, at_b, :], b_scr[dr, at_b, :]
        h_a = a0 * h + b0
        h_b = (a1 * a0) * h + (a1 * b0 + b1)
        h_scr[dr, at_a, :] = h_a
        h_scr[dr, at_b, :] = h_b
        return h_b

    def chunk(i, carry):
        hf, hr = carry
        cr = jnp.where(i < n_ctx, n_ctx - 1 - i, nck - 1 - (i - n_ctx))
        tf0 = pl.multiple_of(i * tc, tc)
        tr0 = pl.multiple_of(cr * tc, tc)
        gates(0, tf0)
        gates(1, tr0)

        def pair(s, c):
            hf, hr = c
            hf = two_steps(0, hf, 2 * s, 2 * s + 1)
            hr = two_steps(1, hr, tc - 1 - 2 * s, tc - 2 - 2 * s)
            return hf, hr
        hf, hr = lax.fori_loop(0, tc // 2, pair, (hf, hr), unroll=True)
        for b in range(batch):
            hs_scr[b, pl.ds(tf0, tc), :] = hs_scr[b, pl.ds(tf0, tc), :] + h_scr[0, pl.ds(b * pitch, tc), :]
            hs_scr[b, pl.ds(tr0, tc), :] = hs_scr[b, pl.ds(tr0, tc), :] + h_scr[1, pl.ds(b * pitch, tc), :]
        return hf, hr

    zero = jnp.zeros((batch, bs), F32)
    lax.fori_loop(0, nck, chunk, (zero, zero))

    for b in range(batch):
        o_ref[b] = (hs_scr[b] * _silu(z_ref[b])).astype(BF16)


def _lru(p, off_x, off_z, conv_w, conv_b, wr, br, wi, bi, lam, ctx_len):
    batch, t, _ = p.shape
    w = conv_w.shape[1]
    bs = w // LRU_BLOCKS
    assert bs == LANE, "RG-LRU gate block must be one lane tile wide"
    tc = _pick_tile(ctx_len, (256, 128))
    assert (t - ctx_len) % tc == 0 and tc % 2 == 0
    pitch = tc + SUBLANE
    bx, bz = off_x // bs, off_z // bs
    vec = lambda rows: pl.BlockSpec((rows, bs), lambda n: (0, n))
    wspec = pl.BlockSpec((2, 1, bs, bs), lambda n: (0, n, 0, 0))
    return pl.pallas_call(
        functools.partial(_lru_kernel, ctx_len=ctx_len, tc=tc, pitch=pitch),
        grid=(LRU_BLOCKS,),
        in_specs=[
            pl.BlockSpec((batch, t, bs), lambda n: (0, 0, bx + n)),
            pl.BlockSpec((batch, t, bs), lambda n: (0, 0, bz + n)),
            vec(CONV_W), vec(1), wspec, vec(2), wspec, vec(2), vec(2),
        ],
        out_specs=pl.BlockSpec((batch, t, bs), lambda n: (0, 0, n)),
        out_shape=jax.ShapeDtypeStruct((batch, t, w), BF16),
        scratch_shapes=[
            pltpu.VMEM((batch, t, bs), F32),
            pltpu.VMEM((batch, t, bs), F32),
            pltpu.VMEM((2, batch * pitch, bs), F32),
            pltpu.VMEM((2, batch * pitch, bs), F32),
            pltpu.VMEM((2, batch * pitch, bs), F32),
            pltpu.VMEM((CONV_W, t, bs), F32),
        ],
        compiler_params=_cparams(("parallel",)),
        name="rglru",
    )(p, p, conv_w, conv_b.reshape(1, w), wr, br, wi, bi, lam)


def _mlstm_kernel(q_ref, kt_ref, v_ref, gc_ref, gr_ref, bc_ref, br_ref, o_ref, c_scr, m_scr, *, hd, nh):
    dr = pl.program_id(1)
    fwd = dr == 0
    sign = 1 - 2 * dr
    scale = hd ** -0.5
    log2_scale = -0.5 * math.log2(hd)

    @pl.when(pl.program_id(2) == 0)
    def _():
        c_scr[...] = jnp.zeros(c_scr.shape, F32)
        m_scr[...] = jnp.full(m_scr.shape, M_INIT, F32)

    row = lax.broadcasted_iota(jnp.int32, (ML_L, ML_L), 0)
    col = lax.broadcasted_iota(jnp.int32, (ML_L, ML_L), 1)
    keep = (col - row) * sign <= 0
    cum_c = jnp.where(keep, 1.0, 0.0).astype(BF16)
    cum_r = jnp.where((row - col) * sign <= 0, 1.0, 0.0).astype(BF16)

    g_c = gc_ref[0] + bc_ref[...]
    g_r = gr_ref[0, 0] + br_ref[...]
    x1, x2, x3 = _split3(_log_sigmoid(g_c))
    bcum_c = _dot(cum_c, x1) + _dot(cum_c, x2) + _dot(cum_c, x3)
    x1, x2, x3 = _split3(_log_sigmoid(g_r))
    bcum_r = _dot(x1, cum_r) + _dot(x2, cum_r) + _dot(x3, cum_r)

    def pick_c(arr, j):
        return jnp.where(fwd, arr[:, j:j + 1], arr[:, 2 * nh + j:2 * nh + j + 1])

    def pick_r(arr, j):
        return jnp.where(fwd, arr[j:j + 1, :], arr[2 * nh + j:2 * nh + j + 1, :])

    ones_blk = jnp.ones((ML_L, LANE), BF16)

    for h in range(nh):
        sl = slice(h * hd, (h + 1) * hd)
        b_col = pick_c(bcum_c, nh + h)
        b_row = pick_r(bcum_r, nh + h)
        a_row = pick_r(g_r, h) - b_row
        m = m_scr[h:h + 1, 0:1]
        q = q_ref[0, :, sl]
        kt = kt_ref[0, sl, :]
        v = v_ref[0, :, sl]
        v_aug = jnp.concatenate([v, ones_blk], axis=1)

        a_vis = jnp.where(keep, a_row * LOG2_E + log2_scale, -jnp.inf)
        m2 = m * LOG2_E
        mx2 = jnp.maximum(jnp.max(a_vis, axis=1, keepdims=True) - log2_scale, m2)
        mx_b = jnp.broadcast_to(mx2, (ML_L, LANE))
        dm = jnp.exp2(a_vis - jnp.concatenate([mx_b] * (ML_L // LANE), axis=1))
        w_inter_b = jnp.exp2(m2 - mx_b)
        s = _dot(q, kt) * dm
        c_aug = c_scr[h]
        qc = _dot(q, c_aug.astype(BF16))
        sv = _dot(s.astype(BF16), v_aug)
        num = sv[:, :hd] + jnp.concatenate([w_inter_b] * (hd // LANE), axis=1) * qc[:, :hd]
        den = (sv[:, hd:] + w_inter_b * qc[:, hd:])[:, 0:1]
        o_ref[0, 0, :, sl] = num / jnp.maximum(jnp.abs(den), jnp.exp2(-(b_col * LOG2_E + mx2)))

        b_last = jnp.where(fwd, b_row[:, ML_L - 1:ML_L], b_row[:, 0:1])
        m_in = jnp.maximum(m, jnp.max(a_row, axis=1, keepdims=True))
        decay = jnp.exp(m - m_in)
        kw_t = kt.astype(F32) * (jnp.exp(a_row - m_in) * scale)
        c_scr[h] = decay * c_aug + _dot(kw_t.astype(BF16), v_aug)
        m_scr[h:h + 1, :] = jnp.broadcast_to(b_last + m_in, (1, LANE))


def _mlstm(q, kt, v, pg, gate_b, ctx_len):
    batch, t, w = q.shape
    nh = ML_HEADS
    hd = w // nh
    nck = t // ML_L
    n_ctx = ctx_len // ML_L
    ng = 4 * nh
    assert ctx_len % ML_L == 0 and t % ML_L == 0 and ng <= LANE
    g_rows = jnp.swapaxes(pg[:, :, :ng].reshape(batch, nck, ML_L, ng), 2, 3)
    bias = gate_b.reshape(ng)
    bias_c = jnp.concatenate([bias, jnp.zeros((LANE - ng,), F32)]).reshape(1, LANE)
    bias_r = bias.reshape(ng, 1)

    def chunk_of(dr, i):
        rev = jnp.where(i < n_ctx, n_ctx - 1 - i, nck - 1 - (i - n_ctx))
        return jnp.where(dr == 0, i, rev)

    rows_spec = pl.BlockSpec((1, ML_L, w), lambda b, dr, i: (b, chunk_of(dr, i), 0))
    return pl.pallas_call(
        functools.partial(_mlstm_kernel, hd=hd, nh=nh),
        grid=(batch, 2, nck),
        in_specs=[
            rows_spec,
            pl.BlockSpec((1, w, ML_L), lambda b, dr, i: (b, 0, chunk_of(dr, i))),
            rows_spec,
            pl.BlockSpec((1, ML_L, LANE), lambda b, dr, i: (b, chunk_of(dr, i), 0)),
            pl.BlockSpec((1, 1, ng, ML_L), lambda b, dr, i: (b, chunk_of(dr, i), 0, 0)),
            pl.BlockSpec((1, LANE), lambda b, dr, i: (0, 0)),
            pl.BlockSpec((ng, 1), lambda b, dr, i: (0, 0)),
        ],
        out_specs=pl.BlockSpec((1, 1, ML_L, w), lambda b, dr, i: (dr, b, chunk_of(dr, i), 0)),
        out_shape=jax.ShapeDtypeStruct((2, batch, t, w), F32),
        scratch_shapes=[
            pltpu.VMEM((nh, hd, hd + LANE), F32),
            pltpu.VMEM((nh, LANE), F32),
        ],
        compiler_params=_cparams(("parallel", "parallel", "arbitrary")),
        name="mlstm",
    )(q, kt, v, pg, g_rows, bias_c, bias_r)


def _ml_out_kernel(h_ref, o_ref, z_ref, g_ref, y_ref):
    h = _sigmoid(o_ref[0]) * (h_ref[0, 0] + h_ref[1, 0])
    y = h * lax.rsqrt(jnp.mean(h * h, axis=-1, keepdims=True) + EPS) * g_ref[...]
    y_ref[0] = (y * _silu(z_ref[0])).astype(BF16)


def _ml_out(hdirs, p, off_o, off_z, ml_norm):
    _, batch, t, w = hdirs.shape
    hd = w // ML_HEADS
    tm = _pick_tile(t, (1152, 768, 384, 256, 128))
    col = lambda off: pl.BlockSpec((1, tm, hd), lambda b, i, h: (b, i, off // hd + h))
    return pl.pallas_call(
        _ml_out_kernel,
        grid=(batch, t // tm, ML_HEADS),
        in_specs=[
            pl.BlockSpec((2, 1, tm, hd), lambda b, i, h: (0, b, i, h)),
            col(off_o), col(off_z),
            pl.BlockSpec((1, hd), lambda b, i, h: (0, h)),
        ],
        out_specs=pl.BlockSpec((1, tm, hd), lambda b, i, h: (b, i, h)),
        out_shape=jax.ShapeDtypeStruct((batch, t, w), BF16),
        compiler_params=_cparams(("parallel", "parallel", "parallel")),
        name="ml_out",
    )(hdirs, p, p, ml_norm.reshape(1, w))


def _rms_rope(x, g, cos_f, sin_s):
    y = x * lax.rsqrt(jnp.mean(x * x, axis=-1, keepdims=True) + EPS) * g
    return y * cos_f + pltpu.roll(y, ATT_HD // 2, axis=1) * sin_s


def _attn_kernel(q_ref, k_ref, v_ref, z_ref, cq_ref, sq_ref, ck_ref, sk_ref, qn_ref, kn_ref, y_ref,
                 k_scr, v_scr, *, ctx_len, gqa):
    qi = pl.program_id(2)

    @pl.when(qi == 0)
    def _():
        k_scr[...] = _rms_rope(k_ref[0], kn_ref[...], ck_ref[...], sk_ref[...]).astype(BF16)
        v_scr[:, 0:ATT_HD] = v_ref[0].astype(BF16)
        v_scr[:, ATT_HD:] = jnp.ones((v_scr.shape[0], LANE), BF16)

    def attend(nkeys):
        kb = k_scr[0:nkeys, :]
        vb = v_scr[0:nkeys, :]
        tq = q_ref.shape[1]
        for g0 in range(0, gqa, ATT_STACK):
            qs = [(_rms_rope(q_ref[0, :, g * ATT_HD:(g + 1) * ATT_HD], qn_ref[...], cq_ref[...], sq_ref[...])
                   * (ATT_HD ** -0.5 * LOG2_E)).astype(BF16) for g in range(g0, g0 + ATT_STACK)]
            s = _dot_nt(jnp.concatenate(qs, axis=0), kb)
            e = jnp.exp2(s - jnp.max(s, axis=-1, keepdims=True))
            ov = _dot(e.astype(BF16), vb)
            o = ov[:, :ATT_HD] / ov[:, ATT_HD:]
            for j in range(ATT_STACK):
                sl = slice((g0 + j) * ATT_HD, (g0 + j + 1) * ATT_HD)
                y_ref[0, :, sl] = (o[j * tq:(j + 1) * tq, :] * _silu(z_ref[0, :, sl])).astype(BF16)

    @pl.when(qi == 0)
    def _():
        attend(ctx_len)

    @pl.when(qi > 0)
    def _():
        attend(k_scr.shape[0])


def _attention(p, off_q, off_k, off_v, off_z, cos_f, sin_s, q_norm, k_norm, ctx_len):
    batch, t, _ = p.shape
    w = off_k - off_q
    gqa = w // ATT_HD // ATT_KV
    gw = gqa * ATT_HD
    tq = ctx_len
    assert t % tq == 0 and off_q % gw == 0 and off_z % gw == 0
    qz = lambda off: pl.BlockSpec((1, tq, gw), lambda b, kv, i: (b, i, off // gw + kv))
    kvs = lambda off: pl.BlockSpec((1, t, ATT_HD), lambda b, kv, i: (b, 0, off // ATT_HD + kv))
    tab_q = pl.BlockSpec((tq, ATT_HD), lambda b, kv, i: (i, 0))
    tab_k = pl.BlockSpec((t, ATT_HD), lambda b, kv, i: (0, 0))
    nrm = pl.BlockSpec((1, ATT_HD), lambda b, kv, i: (0, 0))
    return pl.pallas_call(
        functools.partial(_attn_kernel, ctx_len=ctx_len, gqa=gqa),
        grid=(batch, ATT_KV, t // tq),
        in_specs=[qz(off_q), kvs(off_k), kvs(off_v), qz(off_z), tab_q, tab_q, tab_k, tab_k, nrm, nrm],
        out_specs=pl.BlockSpec((1, tq, gw), lambda b, kv, i: (b, i, kv)),
        out_shape=jax.ShapeDtypeStruct((batch, t, w), BF16),
        scratch_shapes=[pltpu.VMEM((t, ATT_HD), BF16), pltpu.VMEM((t, ATT_HD + LANE), BF16)],
        compiler_params=_cparams(("parallel", "parallel", "arbitrary")),
        name="attention",
    )(p, p, p, p, cos_f, sin_s, cos_f, sin_s, q_norm.reshape(1, ATT_HD), k_norm.reshape(1, ATT_HD))


def _merge_kernel(y0_ref, y1_ref, y2_ref, w_ref, g0_ref, g1_ref, g2_ref, o_ref):
    acc = None
    for n, (y_ref, g_ref) in enumerate(((y0_ref, g0_ref), (y1_ref, g1_ref), (y2_ref, g2_ref))):
        term = _sigmoid(g_ref[0]) * _dot(y_ref[0], w_ref[0, n].astype(BF16))
        acc = term if acc is None else acc + term
    o_ref[0] = acc.astype(BF16)


def _merge(ys, w_br, l, p, off_g):
    batch, t, w = ys[0].shape
    d = w_br.shape[3]
    tm = _pick_tile(t, (1152, 768, 384, 256, 128))
    tn = _pick_tile(d, (256, 128))
    assert off_g % tn == 0
    yspec = pl.BlockSpec((1, tm, w), lambda b, i, j: (b, i, 0))
    gspec = lambda n: pl.BlockSpec((1, tm, tn), lambda b, i, j: (b, i, (off_g + n * d) // tn + j))
    return pl.pallas_call(
        _merge_kernel,
        grid=(batch, t // tm, d // tn),
        in_specs=[yspec, yspec, yspec,
                  pl.BlockSpec((1, N_BRANCH, w, tn), lambda b, i, j: (l, 0, 0, j)),
                  gspec(0), gspec(1), gspec(2)],
        out_specs=pl.BlockSpec((1, tm, tn), lambda b, i, j: (b, i, j)),
        out_shape=jax.ShapeDtypeStruct((batch, t, d), BF16),
        compiler_params=_cparams(("parallel", "parallel", "arbitrary")),
        name="merge",
    )(*ys, w_br, p, p, p)


def _out_kernel(s_ref, w_ref, x_ref, mod_ref, g_ref, o_ref, w_scr, *, ctx_len, batch, d, tile0):
    b, i = pl.program_id(0), pl.program_id(1)
    tm = s_ref.shape[1]

    @pl.when((b == 0) & (i == 0))
    def _():
        rows = 256
        for r in range(0, d, rows):
            w_scr[r:r + rows, :] = w_ref[0, r:r + rows, :].astype(BF16)

    y = _dot(s_ref[0], w_scr[...])
    yn = y * lax.rsqrt(jnp.mean(y * y, axis=-1, keepdims=True) + EPS) * g_ref[...]
    gate = _row_select(i + tile0, tm, ctx_len, b, batch, mod_ref, 2 * d, 3 * d)
    o_ref[0] = x_ref[0] + gate * yn


def _out_proj(zsum, w_out, l, xu, mod_l, g, ctx_len, row0):
    batch, t, d = xu.shape
    tm = _pick_tile(t, (384, 256, 128)) if row0 == 0 else _pick_tile(math.gcd(row0, t - row0), (256, 128))
    tile0 = row0 // tm
    rows_in = pl.BlockSpec((1, tm, d), lambda b, i: (b, i + tile0, 0))
    return pl.pallas_call(
        functools.partial(_out_kernel, ctx_len=ctx_len, batch=batch, d=d, tile0=tile0),
        grid=(batch, (t - row0) // tm),
        in_specs=[
            rows_in,
            pl.BlockSpec((1, d, d), lambda b, i: (l, 0, 0), pipeline_mode=pl.Buffered(1)),
            rows_in,
            pl.BlockSpec((8, 3 * d), lambda b, i: (0, 0)),
            pl.BlockSpec((1, d), lambda b, i: (0, 0)),
        ],
        out_specs=pl.BlockSpec((1, tm, d), lambda b, i: (b, i, 0)),
        out_shape=jax.ShapeDtypeStruct((batch, t - row0, d), F32),
        scratch_shapes=[pltpu.VMEM((d, d), BF16)],
        compiler_params=_cparams(("arbitrary", "arbitrary")),
        name="out_proj",
    )(zsum, w_out, xu, mod_l, g.reshape(1, d))


def _rope_tables(seq, ctx_len):
    rows = seq // GRID_W
    row = jnp.repeat(jnp.arange(rows, dtype=jnp.int32), GRID_W).astype(F32)
    col = jnp.tile(jnp.arange(GRID_W, dtype=jnp.int32), rows).astype(F32)
    n_freq = ATT_HD // 4
    inv = 1.0 / (ROPE_THETA ** (jnp.arange(n_freq, dtype=F32) / n_freq))
    ang = jnp.concatenate([row[:, None] * inv, col[:, None] * inv], axis=-1)
    cos, sin = jnp.cos(ang), jnp.sin(ang)
    cos_f = jnp.concatenate([cos, cos], axis=-1)
    sin_s = jnp.concatenate([-sin, sin], axis=-1)
    cos_f = jnp.concatenate([jnp.ones((ctx_len, ATT_HD), F32), cos_f], axis=0)
    sin_s = jnp.concatenate([jnp.zeros((ctx_len, ATT_HD), F32), sin_s], axis=0)
    return cos_f, sin_s


def kernel(x, c, ctx, c_ctx, ada_w, ada_b, norm_pre, norm_post, w_in, lru_conv_w, lru_conv_b, lru_wr, lru_br, lru_wi, lru_bi, lru_lam, ml_gate_b, ml_norm, q_norm, k_norm, w_br, w_out):
    batch, seq, d = x.shape
    ctx_len = ctx.shape[1]
    t = ctx_len + seq
    depth = ada_w.shape[0]
    w = d
    w_kv = ATT_KV * ATT_HD
    n_gate = 4 * ML_HEADS
    assert batch + 1 <= 8

    c_lru, c_qkv, c_oz, c_gate = 0, 2 * w, 5 * w, 7 * w
    n_att = 2 * w + 2 * w_kv + N_BRANCH * d
    tn = _pick_tile(n_att, (1024, 512, 256))
    assert w % tn == 0
    off_aq, off_ak, off_av, off_az, off_mg = 0, w, w + w_kv, w + 2 * w_kv, 2 * w + 2 * w_kv

    cos_f, sin_s = _rope_tables(seq, ctx_len)
    cc = jnp.concatenate([c, c_ctx[None], jnp.zeros((8 - batch - 1, d), F32)], axis=0)
    mod = _adaln(cc, ada_w, ada_b)
    xu = jnp.concatenate([ctx, x], axis=1)

    for l in range(depth):
        h3 = _normmod(xu, mod[l], norm_pre[l], ctx_len)
        h2 = h3.reshape(batch * t, d)
        proj = lambda c0, n, tile, shift, dt: _in_proj(h2, w_in, l, c0, n, tile, shift, dt).reshape(batch, t, n)
        p_lru = proj(c_lru, 2 * w, tn, 0, F32)
        p_q = proj(c_qkv, w, tn, 0, BF16)
        p_kt = _in_proj_t(h3, w_in[l][:, c_qkv + w:c_qkv + 2 * w].T.astype(BF16))
        p_v = proj(c_qkv + 2 * w, w, tn, 0, BF16)
        p_oz = proj(c_oz, 2 * w, tn, 0, F32)
        p_gate = proj(c_gate, LANE, LANE, 0, F32)
        p_att = proj(c_gate, n_att, tn, n_gate, F32)

        y_lru = _lru(p_lru, 0, w, lru_conv_w[l], lru_conv_b[l], lru_wr[l], lru_br[l],
                     lru_wi[l], lru_bi[l], lru_lam[l], ctx_len)
        hdirs = _mlstm(p_q, p_kt, p_v, p_gate, ml_gate_b[l], ctx_len)
        y_ml = _ml_out(hdirs, p_oz, 0, w, ml_norm[l])
        y_att = _attention(p_att, off_aq, off_ak, off_av, off_az, cos_f, sin_s, q_norm[l], k_norm[l], ctx_len)

        zsum = _merge((y_lru, y_ml, y_att), w_br, l, p_att, off_mg)
        xu = _out_proj(zsum, w_out, l, xu, mod[l], norm_post[l], ctx_len, ctx_len if l == depth - 1 else 0)

    return xu
```

```python
import functools
import math

import jax
import jax.numpy as jnp
from jax import lax
from jax.experimental import pallas as pl
from jax.experimental.pallas import tpu as pltpu

F32 = jnp.float32
BF16 = jnp.bfloat16

EPS = 1e-6
N_BRANCH = 3
LRU_BLOCKS = 16
CONV_W = 4
CONV_PAD_L = 2
LRU_C = 8.0
ML_HEADS = 8
ML_L = 256
M_INIT = -1e30
ATT_HD = 128
ATT_KV = 4
ATT_STACK = 1
GRID_W = 64
ROPE_THETA = 10000.0
LOG2_E = 1.4426950408889634
LANE = 128
SUBLANE = 8
VMEM_LIMIT = 56 * 1024 * 1024


def _cparams(sem):
    return pltpu.CompilerParams(dimension_semantics=sem, vmem_limit_bytes=VMEM_LIMIT)


def _sigmoid(x):
    return 0.5 * jnp.tanh(0.5 * x) + 0.5


def _silu(x):
    return x * _sigmoid(x)


def _log_sigmoid(x):
    return jnp.minimum(x, 0.0) - jnp.log1p(jnp.exp(-jnp.abs(x)))


def _dot(a, b):
    return jnp.dot(a, b, preferred_element_type=F32)


def _dot_nt(a, b):
    return lax.dot_general(a, b, (((1,), (1,)), ((), ())), preferred_element_type=F32)


def _dot_tn(a, b):
    return lax.dot_general(a, b, (((0,), (0,)), ((), ())), preferred_element_type=F32)


def _split3(x):
    x1 = x.astype(BF16)
    r1 = x - x1.astype(F32)
    x2 = r1.astype(BF16)
    x3 = (r1 - x2.astype(F32)).astype(BF16)
    return x1, x2, x3


def _pick_tile(n, cands):
    for c in cands:
        if n % c == 0:
            return c
    raise ValueError(f"no tile for {n} among {cands}")


def _adaln_kernel(c_ref, w_ref, b_ref, o_ref):
    @pl.when(pl.program_id(1) == 0)
    def _():
        o_ref[0] = jnp.broadcast_to(b_ref[0], o_ref.shape[1:])

    s = _silu(c_ref[...]).astype(BF16)
    o_ref[0] += _dot(s, w_ref[0].astype(BF16))


def _adaln(cc, ada_w, ada_b):
    depth, d, n3 = ada_w.shape
    tk = _pick_tile(d, (256, 128))
    return pl.pallas_call(
        _adaln_kernel,
        grid=(depth, d // tk),
        in_specs=[
            pl.BlockSpec((8, tk), lambda l, k: (0, k)),
            pl.BlockSpec((1, tk, n3), lambda l, k: (l, k, 0)),
            pl.BlockSpec((1, 1, n3), lambda l, k: (l, 0, 0)),
        ],
        out_specs=pl.BlockSpec((1, 8, n3), lambda l, k: (l, 0, 0)),
        out_shape=jax.ShapeDtypeStruct((depth, 8, n3), F32),
        compiler_params=_cparams(("parallel", "arbitrary")),
        name="adaln",
    )(cc, ada_w, ada_b.reshape(depth, 1, n3))


def _row_select(i, tm, ctx_len, b, batch, mod_ref, lo, hi):
    row = i * tm + lax.broadcasted_iota(jnp.int32, (tm, 1), 0)
    vx = mod_ref[pl.ds(b, 1), lo:hi]
    vc = mod_ref[batch:batch + 1, lo:hi]
    return jnp.where(row < ctx_len, vc, vx)


def _norm_modulate(x, i, tm, ctx_len, b, batch, mod_ref, g_ref, d):
    y = x * lax.rsqrt(jnp.mean(x * x, axis=-1, keepdims=True) + EPS) * g_ref[...]
    shift = _row_select(i, tm, ctx_len, b, batch, mod_ref, 0, d)
    scale = _row_select(i, tm, ctx_len, b, batch, mod_ref, d, 2 * d)
    return (y * (1.0 + scale) + shift).astype(BF16)


def _normmod_kernel(x_ref, mod_ref, g_ref, o_ref, *, ctx_len, batch, d):
    b, i = pl.program_id(0), pl.program_id(1)
    o_ref[0] = _norm_modulate(x_ref[0], i, x_ref.shape[1], ctx_len, b, batch, mod_ref, g_ref, d)


def _normmod(xu, mod_l, g, ctx_len):
    batch, t, d = xu.shape
    tm = _pick_tile(t, (768, 384, 256, 128))
    return pl.pallas_call(
        functools.partial(_normmod_kernel, ctx_len=ctx_len, batch=batch, d=d),
        grid=(batch, t // tm),
        in_specs=[
            pl.BlockSpec((1, tm, d), lambda b, i: (b, i, 0)),
            pl.BlockSpec((8, 3 * d), lambda b, i: (0, 0)),
            pl.BlockSpec((1, d), lambda b, i: (0, 0)),
        ],
        out_specs=pl.BlockSpec((1, tm, d), lambda b, i: (b, i, 0)),
        out_shape=jax.ShapeDtypeStruct((batch, t, d), BF16),
        compiler_params=_cparams(("parallel", "parallel")),
        name="normmod",
    )(xu, mod_l, g.reshape(1, d))


def _load_w_tile(w_ref, w2_ref, w_scr, shift):
    tn = w_ref.shape[1]
    rows = 256
    for r in range(0, tn - shift, rows):
        n = min(rows, tn - shift - r)
        w_scr[r:r + n, :] = w_ref[0, shift + r:shift + r + n, :].astype(BF16)
    if shift:
        w_scr[tn - shift:tn, :] = w2_ref[0].astype(BF16)


def _in_proj_kernel(h_ref, w_ref, w2_ref, o_ref, w_scr, *, shift):
    @pl.when(pl.program_id(1) == 0)
    def _():
        _load_w_tile(w_ref, w2_ref, w_scr, shift)

    o_ref[...] = _dot_nt(h_ref[...], w_scr[...]).astype(o_ref.dtype)


def _w_specs(l, c0, tn, shift, k, jmap):
    extra = shift if shift else SUBLANE
    return [
        pl.BlockSpec((1, tn, k), lambda *g: (l, c0 // tn + jmap(*g), 0)),
        pl.BlockSpec((1, extra, k), lambda *g: (l, (c0 + (jmap(*g) + 1) * tn) // extra if shift else 0, 0)),
    ]


def _in_proj(h2, wt, l, c0, ncols, tn, shift, out_dtype):
    m, k = h2.shape
    tm = _pick_tile(m, (1152, 768, 512, 384, 256, 128))
    assert c0 % tn == 0 and ncols % tn == 0 and tn % LANE == 0 and tn % max(shift, 1) == 0 and shift % 16 == 0
    return pl.pallas_call(
        functools.partial(_in_proj_kernel, shift=shift),
        grid=(ncols // tn, m // tm),
        in_specs=[pl.BlockSpec((tm, k), lambda j, i: (i, 0))] + _w_specs(l, c0, tn, shift, k, lambda j, i: j),
        out_specs=pl.BlockSpec((tm, tn), lambda j, i: (i, j)),
        out_shape=jax.ShapeDtypeStruct((m, ncols), out_dtype),
        scratch_shapes=[pltpu.VMEM((tn, k), BF16)],
        compiler_params=_cparams(("parallel", "arbitrary")),
        name="in_proj",
    )(h2, wt, wt)


def _in_proj_t_kernel(h_ref, w_ref, w2_ref, o_ref, w_scr):
    @pl.when((pl.program_id(1) == 0) & (pl.program_id(2) == 0))
    def _():
        _load_w_tile(w_ref, w2_ref, w_scr, 0)

    o_ref[0] = _dot_nt(w_scr[...], h_ref[0]).astype(o_ref.dtype)


def _in_proj_t(h, wt, l, c0, ncols, tn):
    batch, t, k = h.shape
    tm = _pick_tile(t, (768, 384, 256, 128))
    assert c0 % tn == 0 and ncols % tn == 0
    return pl.pallas_call(
        _in_proj_t_kernel,
        grid=(ncols // tn, batch, t // tm),
        in_specs=[pl.BlockSpec((1, tm, k), lambda j, b, i: (b, i, 0))]
        + _w_specs(l, c0, tn, 0, k, lambda j, b, i: j),
        out_specs=pl.BlockSpec((1, tn, tm), lambda j, b, i: (b, j, i)),
        out_shape=jax.ShapeDtypeStruct((batch, ncols, t), BF16),
        scratch_shapes=[pltpu.VMEM((tn, k), BF16)],
        compiler_params=_cparams(("parallel", "arbitrary", "arbitrary")),
        name="in_proj_t",
    )(h, wt, wt)


def _lru_kernel(x_ref, z_ref, cw_ref, cb_ref, wr_ref, br_ref, wi_ref, bi_ref, lam_ref, o_ref,
                xc_scr, hs_scr, a_scr, b_scr, h_scr, cwm_scr, *, ctx_len, tc, pitch):
    batch, t, bs = x_ref.shape
    nck = t // tc
    n_ctx = ctx_len // tc

    trow = lax.broadcasted_iota(jnp.int32, (t, 1), 0)
    seg = jnp.where(trow < ctx_len, trow, trow - ctx_len)
    seg_len = jnp.where(trow < ctx_len, ctx_len, t - ctx_len)
    cw = cw_ref[...]
    taps = [k for k in range(CONV_W) if k != CONV_PAD_L]
    for k in taps:
        off = k - CONV_PAD_L
        ok = (seg + off >= 0) & (seg + off < seg_len)
        cwm_scr[k] = jnp.where(ok, cw[k:k + 1], 0.0)
    cw_c = cw[CONV_PAD_L:CONV_PAD_L + 1]
    edge = 2 * SUBLANE
    for b in range(batch):
        lo, hi = SUBLANE, t - SUBLANE
        acc = cb_ref[...] + x_ref[b, lo:hi, :] * cw_c
        for k in taps:
            off = k - CONV_PAD_L
            acc = acc + x_ref[b, lo + off:hi + off, :] * cwm_scr[k, lo:hi, :]
        xc_scr[b, lo:hi, :] = acc
        for s0, keep_lo in ((0, 0), (t - edge, SUBLANE)):
            e = x_ref[b, s0:s0 + edge, :]
            acc = cb_ref[...] + e * cw_c
            for k in taps:
                acc = acc + pltpu.roll(e, (CONV_PAD_L - k) % edge, axis=0) * cwm_scr[k, s0:s0 + edge, :]
            xc_scr[b, s0 + keep_lo:s0 + keep_lo + SUBLANE, :] = acc[keep_lo:keep_lo + SUBLANE]
        hs_scr[b] = jnp.zeros((t, bs), F32)

    wr = [(0.5 * wr_ref[dr, 0]).astype(BF16) for dr in range(2)]
    wi = [(0.5 * wi_ref[dr, 0]).astype(BF16) for dr in range(2)]
    br_h = 0.5 * br_ref[...]
    bi_h = 0.5 * bi_ref[...]
    lam = lam_ref[...]
    sp = jnp.maximum(-lam, 0.0) + jnp.log1p(jnp.exp(-jnp.abs(lam)))
    la_c = (-0.5 * LRU_C) * sp

    def chunk_rows(i):
        cr = jnp.where(i < n_ctx, n_ctx - 1 - i, nck - 1 - (i - n_ctx))
        return pl.multiple_of(i * tc, tc), pl.multiple_of(cr * tc, tc)

    def gates(i):
        for dr, t0 in enumerate(chunk_rows(i)):
            for b in range(batch):
                xc = xc_scr[b, pl.ds(t0, tc), :]
                xb = xc.astype(BF16)
                th_r = jnp.tanh(_dot(xb, wr[dr]) + br_h[dr:dr + 1, :])
                th_i = jnp.tanh(_dot(xb, wi[dr]) + bi_h[dr:dr + 1, :])
                log_a = la_c[dr:dr + 1, :] * th_r + la_c[dr:dr + 1, :]
                a_scr[dr, pl.ds(b * pitch, tc), :] = jnp.exp(log_a)
                u = jnp.abs(jnp.tanh(log_a))
                mult = lax.rsqrt(0.5 / u + 0.5)
                b_scr[dr, pl.ds(b * pitch, tc), :] = mult * ((th_i + 1.0) * (0.5 * xc))

    def two_steps(dr, h, t_a, t_b):
        at_a = pl.ds(t_a, batch, stride=pitch)
        at_b = pl.ds(t_b, batch, stride=pitch)
        a0, b0 = a_scr[dr, at_a, :], b_scr[dr, at_a, :]
        a1, b1 = a_scr[dr, at_b, :], b_scr[dr, at_b, :]
        h_a = a0 * h + b0
        h_b = (a1 * a0) * h + (a1 * b0 + b1)
        h_scr[dr, at_a, :] = h_a
        h_scr[dr, at_b, :] = h_b
        return h_b

    def chunk(i, carry):
        hf, hr = carry
        gates(i)
        for s in range(tc // 2):
            hf = two_steps(0, hf, 2 * s, 2 * s + 1)
            hr = two_steps(1, hr, tc - 1 - 2 * s, tc - 2 - 2 * s)
        for dr, t0 in enumerate(chunk_rows(i)):
            for b in range(batch):
                hs_scr[b, pl.ds(t0, tc), :] = hs_scr[b, pl.ds(t0, tc), :] + h_scr[dr, pl.ds(b * pitch, tc), :]
        return hf, hr

    zero = jnp.zeros((batch, bs), F32)
    lax.fori_loop(0, nck, chunk, (zero, zero))

    for b in range(batch):
        o_ref[b] = (hs_scr[b] * _silu(z_ref[b])).astype(BF16)


def _lru(p, off_x, off_z, conv_w, conv_b, wr, br, wi, bi, lam, ctx_len):
    batch, t, _ = p.shape
    w = conv_w.shape[1]
    bs = w // LRU_BLOCKS
    assert bs == LANE, "RG-LRU gate block must be one lane tile wide"
    tc = _pick_tile(ctx_len, (256, 128))
    assert (t - ctx_len) % tc == 0 and tc % 2 == 0
    pitch = tc + SUBLANE
    bx, bz = off_x // bs, off_z // bs
    vec = lambda rows: pl.BlockSpec((rows, bs), lambda n: (0, n))
    wspec = pl.BlockSpec((2, 1, bs, bs), lambda n: (0, n, 0, 0))
    return pl.pallas_call(
        functools.partial(_lru_kernel, ctx_len=ctx_len, tc=tc, pitch=pitch),
        grid=(LRU_BLOCKS,),
        in_specs=[
            pl.BlockSpec((batch, t, bs), lambda n: (0, 0, bx + n)),
            pl.BlockSpec((batch, t, bs), lambda n: (0, 0, bz + n)),
            vec(CONV_W), vec(1), wspec, vec(2), wspec, vec(2), vec(2),
        ],
        out_specs=pl.BlockSpec((batch, t, bs), lambda n: (0, 0, n)),
        out_shape=jax.ShapeDtypeStruct((batch, t, w), BF16),
        scratch_shapes=[
            pltpu.VMEM((batch, t, bs), F32),
            pltpu.VMEM((batch, t, bs), F32),
            pltpu.VMEM((2, batch * pitch, bs), F32),
            pltpu.VMEM((2, batch * pitch, bs), F32),
            pltpu.VMEM((2, batch * pitch, bs), F32),
            pltpu.VMEM((CONV_W, t, bs), F32),
        ],
        compiler_params=_cparams(("parallel",)),
        name="rglru",
    )(p, p, conv_w, conv_b.reshape(1, w), wr, br, wi, bi, lam)


def _mlstm_kernel(q_ref, kt_ref, v_ref, gc_ref, gr_ref, bc_ref, br_ref, o_ref, c_scr, m_scr, *, hd, nh):
    dr = pl.program_id(1)
    fwd = dr == 0
    sign = 1 - 2 * dr
    scale = hd ** -0.5
    log2_scale = -0.5 * math.log2(hd)

    @pl.when(pl.program_id(2) == 0)
    def _():
        c_scr[...] = jnp.zeros(c_scr.shape, F32)
        m_scr[...] = jnp.full(m_scr.shape, M_INIT, F32)

    row = lax.broadcasted_iota(jnp.int32, (ML_L, ML_L), 0)
    col = lax.broadcasted_iota(jnp.int32, (ML_L, ML_L), 1)
    keep = (col - row) * sign <= 0
    cum_c = jnp.where(keep, 1.0, 0.0).astype(BF16)
    cum_r = jnp.where((row - col) * sign <= 0, 1.0, 0.0).astype(BF16)

    g_c = gc_ref[0] + bc_ref[...]
    g_r = gr_ref[0, 0] + br_ref[...]
    x1, x2, x3 = _split3(_log_sigmoid(g_c))
    bcum_c = _dot(cum_c, x1) + _dot(cum_c, x2) + _dot(cum_c, x3)
    x1, x2, x3 = _split3(_log_sigmoid(g_r))
    bcum_r = _dot(x1, cum_r) + _dot(x2, cum_r) + _dot(x3, cum_r)

    def pick_c(arr, j):
        return jnp.where(fwd, arr[:, j:j + 1], arr[:, 2 * nh + j:2 * nh + j + 1])

    def pick_r(arr, j):
        return jnp.where(fwd, arr[j:j + 1, :], arr[2 * nh + j:2 * nh + j + 1, :])

    ones_blk = jnp.ones((ML_L, LANE), BF16)

    for h in range(nh):
        sl = slice(h * hd, (h + 1) * hd)
        b_col = pick_c(bcum_c, nh + h)
        b_row = pick_r(bcum_r, nh + h)
        a_row = pick_r(g_r, h) - b_row
        m = m_scr[h:h + 1, 0:1]
        q = q_ref[0, :, sl]
        kt = kt_ref[0, sl, :]
        v = v_ref[0, :, sl]
        v_aug = jnp.concatenate([v, ones_blk], axis=1)

        a_vis = jnp.where(keep, a_row * LOG2_E + log2_scale, -jnp.inf)
        m2 = m * LOG2_E
        mx2 = jnp.maximum(jnp.max(a_vis, axis=1, keepdims=True) - log2_scale, m2)
        mx_b = jnp.broadcast_to(mx2, (ML_L, LANE))
        dm = jnp.exp2(a_vis - jnp.concatenate([mx_b] * (ML_L // LANE), axis=1))
        w_inter_b = jnp.exp2(m2 - mx_b)
        s = _dot(q, kt) * dm
        c_aug = c_scr[h]
        qc = _dot(q, c_aug.astype(BF16))
        sv = _dot(s.astype(BF16), v_aug)
        num = sv[:, :hd] + jnp.concatenate([w_inter_b] * (hd // LANE), axis=1) * qc[:, :hd]
        den = (sv[:, hd:] + w_inter_b * qc[:, hd:])[:, 0:1]
        o_ref[0, 0, :, sl] = num / jnp.maximum(jnp.abs(den), jnp.exp2(-(b_col * LOG2_E + mx2)))

        b_last = jnp.where(fwd, b_row[:, ML_L - 1:ML_L], b_row[:, 0:1])
        m_in = jnp.maximum(m, jnp.max(a_row, axis=1, keepdims=True))
        decay = jnp.exp(m - m_in)
        kw_t = kt.astype(F32) * (jnp.exp(a_row - m_in) * scale)
        c_scr[h] = decay * c_aug + _dot(kw_t.astype(BF16), v_aug)
        m_scr[h:h + 1, :] = jnp.broadcast_to(b_last + m_in, (1, LANE))


def _mlstm(q, kt, v, pg, gate_b, ctx_len):
    batch, t, w = q.shape
    nh = ML_HEADS
    hd = w // nh
    nck = t // ML_L
    n_ctx = ctx_len // ML_L
    ng = 4 * nh
    assert ctx_len % ML_L == 0 and t % ML_L == 0 and ng <= LANE
    g_rows = jnp.swapaxes(pg[:, :, :ng].reshape(batch, nck, ML_L, ng), 2, 3)
    bias = gate_b.reshape(ng)
    bias_c = jnp.concatenate([bias, jnp.zeros((LANE - ng,), F32)]).reshape(1, LANE)
    bias_r = bias.reshape(ng, 1)

    def chunk_of(dr, i):
        rev = jnp.where(i < n_ctx, n_ctx - 1 - i, nck - 1 - (i - n_ctx))
        return jnp.where(dr == 0, i, rev)

    rows_spec = pl.BlockSpec((1, ML_L, w), lambda b, dr, i: (b, chunk_of(dr, i), 0))
    return pl.pallas_call(
        functools.partial(_mlstm_kernel, hd=hd, nh=nh),
        grid=(batch, 2, nck),
        in_specs=[
            rows_spec,
            pl.BlockSpec((1, w, ML_L), lambda b, dr, i: (b, 0, chunk_of(dr, i))),
            rows_spec,
            pl.BlockSpec((1, ML_L, LANE), lambda b, dr, i: (b, chunk_of(dr, i), 0)),
            pl.BlockSpec((1, 1, ng, ML_L), lambda b, dr, i: (b, chunk_of(dr, i), 0, 0)),
            pl.BlockSpec((1, LANE), lambda b, dr, i: (0, 0)),
            pl.BlockSpec((ng, 1), lambda b, dr, i: (0, 0)),
        ],
        out_specs=pl.BlockSpec((1, 1, ML_L, w), lambda b, dr, i: (dr, b, chunk_of(dr, i), 0)),
        out_shape=jax.ShapeDtypeStruct((2, batch, t, w), F32),
        scratch_shapes=[
            pltpu.VMEM((nh, hd, hd + LANE), F32),
            pltpu.VMEM((nh, LANE), F32),
        ],
        compiler_params=_cparams(("parallel", "parallel", "arbitrary")),
        name="mlstm",
    )(q, kt, v, pg, g_rows, bias_c, bias_r)


def _ml_out_kernel(h_ref, o_ref, z_ref, g_ref, y_ref):
    h = _sigmoid(o_ref[0]) * (h_ref[0, 0] + h_ref[1, 0])
    y = h * lax.rsqrt(jnp.mean(h * h, axis=-1, keepdims=True) + EPS) * g_ref[...]
    y_ref[0] = (y * _silu(z_ref[0])).astype(BF16)


def _ml_out(hdirs, p, off_o, off_z, ml_norm):
    _, batch, t, w = hdirs.shape
    hd = w // ML_HEADS
    tm = _pick_tile(t, (1152, 768, 384, 256, 128))
    col = lambda off: pl.BlockSpec((1, tm, hd), lambda b, i, h: (b, i, off // hd + h))
    return pl.pallas_call(
        _ml_out_kernel,
        grid=(batch, t // tm, ML_HEADS),
        in_specs=[
            pl.BlockSpec((2, 1, tm, hd), lambda b, i, h: (0, b, i, h)),
            col(off_o), col(off_z),
            pl.BlockSpec((1, hd), lambda b, i, h: (0, h)),
        ],
        out_specs=pl.BlockSpec((1, tm, hd), lambda b, i, h: (b, i, h)),
        out_shape=jax.ShapeDtypeStruct((batch, t, w), BF16),
        compiler_params=_cparams(("parallel", "parallel", "parallel")),
        name="ml_out",
    )(hdirs, p, p, ml_norm.reshape(1, w))


def _rms_rope(x, g, cos_f, sin_s):
    y = x * lax.rsqrt(jnp.mean(x * x, axis=-1, keepdims=True) + EPS) * g
    return y * cos_f + pltpu.roll(y, ATT_HD // 2, axis=1) * sin_s


def _attn_kernel(q_ref, k_ref, v_ref, z_ref, cq_ref, sq_ref, ck_ref, sk_ref, qn_ref, kn_ref, y_ref,
                 k_scr, v_scr, *, ctx_len, gqa):
    qi = pl.program_id(2)

    @pl.when(qi == 0)
    def _():
        k_scr[...] = _rms_rope(k_ref[0], kn_ref[...], ck_ref[...], sk_ref[...]).astype(BF16)
        v_scr[:, 0:ATT_HD] = v_ref[0].astype(BF16)
        v_scr[:, ATT_HD:] = jnp.ones((v_scr.shape[0], LANE), BF16)

    def attend(nkeys):
        kb = k_scr[0:nkeys, :]
        vb = v_scr[0:nkeys, :]
        tq = q_ref.shape[1]
        for g0 in range(0, gqa, ATT_STACK):
            qs = [(_rms_rope(q_ref[0, :, g * ATT_HD:(g + 1) * ATT_HD], qn_ref[...], cq_ref[...], sq_ref[...])
                   * (ATT_HD ** -0.5 * LOG2_E)).astype(BF16) for g in range(g0, g0 + ATT_STACK)]
            s = _dot_nt(jnp.concatenate(qs, axis=0), kb)
            e = jnp.exp2(s - jnp.max(s, axis=-1, keepdims=True))
            ov = _dot(e.astype(BF16), vb)
            o = ov[:, :ATT_HD] / ov[:, ATT_HD:]
            for j in range(ATT_STACK):
                sl = slice((g0 + j) * ATT_HD, (g0 + j + 1) * ATT_HD)
                y_ref[0, :, sl] = (o[j * tq:(j + 1) * tq, :] * _silu(z_ref[0, :, sl])).astype(BF16)

    @pl.when(qi == 0)
    def _():
        attend(ctx_len)

    @pl.when(qi > 0)
    def _():
        attend(k_scr.shape[0])


def _attention(p, off_q, off_k, off_v, off_z, cos_f, sin_s, q_norm, k_norm, ctx_len):
    batch, t, _ = p.shape
    w = off_k - off_q
    gqa = w // ATT_HD // ATT_KV
    gw = gqa * ATT_HD
    tq = ctx_len
    assert t % tq == 0 and off_q % gw == 0 and off_z % gw == 0
    qz = lambda off: pl.BlockSpec((1, tq, gw), lambda b, kv, i: (b, i, off // gw + kv))
    kvs = lambda off: pl.BlockSpec((1, t, ATT_HD), lambda b, kv, i: (b, 0, off // ATT_HD + kv))
    tab_q = pl.BlockSpec((tq, ATT_HD), lambda b, kv, i: (i, 0))
    tab_k = pl.BlockSpec((t, ATT_HD), lambda b, kv, i: (0, 0))
    nrm = pl.BlockSpec((1, ATT_HD), lambda b, kv, i: (0, 0))
    return pl.pallas_call(
        functools.partial(_attn_kernel, ctx_len=ctx_len, gqa=gqa),
        grid=(batch, ATT_KV, t // tq),
        in_specs=[qz(off_q), kvs(off_k), kvs(off_v), qz(off_z), tab_q, tab_q, tab_k, tab_k, nrm, nrm],
        out_specs=pl.BlockSpec((1, tq, gw), lambda b, kv, i: (b, i, kv)),
        out_shape=jax.ShapeDtypeStruct((batch, t, w), BF16),
        scratch_shapes=[pltpu.VMEM((t, ATT_HD), BF16), pltpu.VMEM((t, ATT_HD + LANE), BF16)],
        compiler_params=_cparams(("parallel", "parallel", "arbitrary")),
        name="attention",
    )(p, p, p, p, cos_f, sin_s, cos_f, sin_s, q_norm.reshape(1, ATT_HD), k_norm.reshape(1, ATT_HD))


def _merge_kernel(y0_ref, y1_ref, y2_ref, w_ref, g0_ref, g1_ref, g2_ref, o_ref):
    acc = None
    for n, (y_ref, g_ref) in enumerate(((y0_ref, g0_ref), (y1_ref, g1_ref), (y2_ref, g2_ref))):
        term = _sigmoid(g_ref[0]) * _dot(y_ref[0], w_ref[0, n].astype(BF16))
        acc = term if acc is None else acc + term
    o_ref[0] = acc.astype(BF16)


def _merge(ys, w_br, l, p, off_g):
    batch, t, w = ys[0].shape
    d = w_br.shape[3]
    tm = _pick_tile(t, (1152, 768, 384, 256, 128))
    tn = _pick_tile(d, (256, 128))
    assert off_g % tn == 0
    yspec = pl.BlockSpec((1, tm, w), lambda b, i, j: (b, i, 0))
    gspec = lambda n: pl.BlockSpec((1, tm, tn), lambda b, i, j: (b, i, (off_g + n * d) // tn + j))
    return pl.pallas_call(
        _merge_kernel,
        grid=(batch, t // tm, d // tn),
        in_specs=[yspec, yspec, yspec,
                  pl.BlockSpec((1, N_BRANCH, w, tn), lambda b, i, j: (l, 0, 0, j)),
                  gspec(0), gspec(1), gspec(2)],
        out_specs=pl.BlockSpec((1, tm, tn), lambda b, i, j: (b, i, j)),
        out_shape=jax.ShapeDtypeStruct((batch, t, d), BF16),
        compiler_params=_cparams(("parallel", "parallel", "arbitrary")),
        name="merge",
    )(*ys, w_br, p, p, p)


def _out_kernel(s_ref, w_ref, x_ref, mod_ref, g_ref, *rest, ctx_len, batch, d, tile0, with_next):
    if with_next:
        modn_ref, gn_ref, o_ref, hn_ref, w_scr = rest
    else:
        o_ref, w_scr = rest
    b, i = pl.program_id(0), pl.program_id(1)
    tm = s_ref.shape[1]

    @pl.when((b == 0) & (i == 0))
    def _():
        rows = 256
        for r in range(0, d, rows):
            w_scr[r:r + rows, :] = w_ref[0, r:r + rows, :].astype(BF16)

    y = _dot(s_ref[0], w_scr[...])
    yn = y * lax.rsqrt(jnp.mean(y * y, axis=-1, keepdims=True) + EPS) * g_ref[...]
    gate = _row_select(i + tile0, tm, ctx_len, b, batch, mod_ref, 2 * d, 3 * d)
    x = x_ref[0] + gate * yn
    o_ref[0] = x
    if with_next:
        hn_ref[0] = _norm_modulate(x, i + tile0, tm, ctx_len, b, batch, modn_ref, gn_ref, d)


def _out_proj(zsum, w_out, l, xu, mod_l, g, ctx_len, row0, nxt=None):
    batch, t, d = xu.shape
    tm = _pick_tile(t, (384, 256, 128)) if row0 == 0 else _pick_tile(math.gcd(row0, t - row0), (256, 128))
    tile0 = row0 // tm
    rows_in = pl.BlockSpec((1, tm, d), lambda b, i: (b, i + tile0, 0))
    rows_out = pl.BlockSpec((1, tm, d), lambda b, i: (b, i, 0))
    mod_spec = pl.BlockSpec((8, 3 * d), lambda b, i: (0, 0))
    vec_spec = pl.BlockSpec((1, d), lambda b, i: (0, 0))
    with_next = nxt is not None
    extra_in = [mod_spec, vec_spec] if with_next else []
    extra_args = (nxt[0], nxt[1].reshape(1, d)) if with_next else ()
    x_shape = jax.ShapeDtypeStruct((batch, t - row0, d), F32)
    return pl.pallas_call(
        functools.partial(_out_kernel, ctx_len=ctx_len, batch=batch, d=d, tile0=tile0, with_next=with_next),
        grid=(batch, (t - row0) // tm),
        in_specs=[
            rows_in,
            pl.BlockSpec((1, d, d), lambda b, i: (l, 0, 0), pipeline_mode=pl.Buffered(1)),
            rows_in, mod_spec, vec_spec,
        ] + extra_in,
        out_specs=(rows_out, rows_out) if with_next else rows_out,
        out_shape=(x_shape, jax.ShapeDtypeStruct((batch, t - row0, d), BF16)) if with_next else x_shape,
        scratch_shapes=[pltpu.VMEM((d, d), BF16)],
        compiler_params=_cparams(("arbitrary", "arbitrary")),
        name="out_proj",
    )(zsum, w_out, xu, mod_l, g.reshape(1, d), *extra_args)


def _rope_tables(seq, ctx_len):
    rows = seq // GRID_W
    row = jnp.repeat(jnp.arange(rows, dtype=jnp.int32), GRID_W).astype(F32)
    col = jnp.tile(jnp.arange(GRID_W, dtype=jnp.int32), rows).astype(F32)
    n_freq = ATT_HD // 4
    inv = 1.0 / (ROPE_THETA ** (jnp.arange(n_freq, dtype=F32) / n_freq))
    ang = jnp.concatenate([row[:, None] * inv, col[:, None] * inv], axis=-1)
    cos, sin = jnp.cos(ang), jnp.sin(ang)
    cos_f = jnp.concatenate([cos, cos], axis=-1)
    sin_s = jnp.concatenate([-sin, sin], axis=-1)
    cos_f = jnp.concatenate([jnp.ones((ctx_len, ATT_HD), F32), cos_f], axis=0)
    sin_s = jnp.concatenate([jnp.zeros((ctx_len, ATT_HD), F32), sin_s], axis=0)
    return cos_f, sin_s


def kernel(x, c, ctx, c_ctx, ada_w, ada_b, norm_pre, norm_post, w_in, lru_conv_w, lru_conv_b, lru_wr, lru_br, lru_wi, lru_bi, lru_lam, ml_gate_b, ml_norm, q_norm, k_norm, w_br, w_out):
    batch, seq, d = x.shape
    ctx_len = ctx.shape[1]
    t = ctx_len + seq
    depth = ada_w.shape[0]
    w = d
    w_kv = ATT_KV * ATT_HD
    n_gate = 4 * ML_HEADS
    assert batch + 1 <= 8

    c_lru, c_qkv, c_oz, c_gate = 0, 2 * w, 5 * w, 7 * w
    n_att = 2 * w + 2 * w_kv + N_BRANCH * d
    tn = _pick_tile(n_att, (1024, 512, 256))
    assert w % tn == 0
    off_aq, off_ak, off_av, off_az, off_mg = 0, w, w + w_kv, w + 2 * w_kv, 2 * w + 2 * w_kv

    cos_f, sin_s = _rope_tables(seq, ctx_len)
    cc = jnp.concatenate([c, c_ctx[None], jnp.zeros((8 - batch - 1, d), F32)], axis=0)
    mod = _adaln(cc, ada_w, ada_b)
    xu = jnp.concatenate([ctx, x], axis=1)
    wt_in = jnp.swapaxes(w_in, 1, 2)

    h3 = _normmod(xu, mod[0], norm_pre[0], ctx_len)
    for l in range(depth):
        h2 = h3.reshape(batch * t, d)
        proj = lambda c0, n, tile, shift, dt: _in_proj(h2, wt_in, l, c0, n, tile, shift, dt).reshape(batch, t, n)
        p_lru = proj(c_lru, 2 * w, tn, 0, F32)
        p_q = proj(c_qkv, w, tn, 0, BF16)
        p_kt = _in_proj_t(h3, wt_in, l, c_qkv + w, w, tn)
        p_v = proj(c_qkv + 2 * w, w, tn, 0, BF16)
        p_oz = proj(c_oz, 2 * w, tn, 0, F32)
        p_gate = proj(c_gate, LANE, LANE, 0, F32)
        p_att = proj(c_gate, n_att, tn, n_gate, F32)

        y_lru = _lru(p_lru, 0, w, lru_conv_w[l], lru_conv_b[l], lru_wr[l], lru_br[l],
                     lru_wi[l], lru_bi[l], lru_lam[l], ctx_len)
        hdirs = _mlstm(p_q, p_kt, p_v, p_gate, ml_gate_b[l], ctx_len)
        y_ml = _ml_out(hdirs, p_oz, 0, w, ml_norm[l])
        y_att = _attention(p_att, off_aq, off_ak, off_av, off_az, cos_f, sin_s, q_norm[l], k_norm[l], ctx_len)

        zsum = _merge((y_lru, y_ml, y_att), w_br, l, p_att, off_mg)
        if l < depth - 1:
            xu, h3 = _out_proj(zsum, w_out, l, xu, mod[l], norm_post[l], ctx_len, 0,
                               nxt=(mod[l + 1], norm_pre[l + 1]))
        else:
            xu = _out_proj(zsum, w_out, l, xu, mod[l], norm_post[l], ctx_len, ctx_len)

    return xu
```

```python
import functools
import math

import jax
import jax.numpy as jnp
from jax import lax
from jax.experimental import pallas as pl
from jax.experimental.pallas import tpu as pltpu

F32 = jnp.float32
BF16 = jnp.bfloat16

EPS = 1e-6
N_BRANCH = 3
LRU_BLOCKS = 16
CONV_W = 4
CONV_PAD_L = 2
LRU_C = 8.0
ML_HEADS = 8
ML_L = 256
M_INIT = -1e30
ATT_HD = 128
ATT_KV = 4
GRID_W = 64
ROPE_THETA = 10000.0
LOG2_E = 1.4426950408889634
LANE = 128
SUBLANE = 8
VMEM_LIMIT = 56 * 1024 * 1024


def _cparams(sem):
    return pltpu.CompilerParams(dimension_semantics=sem, vmem_limit_bytes=VMEM_LIMIT)


def _sigmoid(x):
    return 0.5 * jnp.tanh(0.5 * x) + 0.5


def _silu(x):
    return x * _sigmoid(x)


def _log_sigmoid(x):
    return jnp.minimum(x, 0.0) - jnp.log1p(jnp.exp(-jnp.abs(x)))


def _dot(a, b):
    return jnp.dot(a, b, preferred_element_type=F32)


def _dot_nt(a, b):
    return lax.dot_general(a, b, (((1,), (1,)), ((), ())), preferred_element_type=F32)


def _dot_tn(a, b):
    return lax.dot_general(a, b, (((0,), (0,)), ((), ())), preferred_element_type=F32)


def _split3(x):
    x1 = x.astype(BF16)
    r1 = x - x1.astype(F32)
    x2 = r1.astype(BF16)
    x3 = (r1 - x2.astype(F32)).astype(BF16)
    return x1, x2, x3


def _pick_tile(n, cands):
    for c in cands:
        if n % c == 0:
            return c
    raise ValueError(f"no tile for {n} among {cands}")


def _adaln_kernel(c_ref, w_ref, b_ref, o_ref):
    @pl.when(pl.program_id(1) == 0)
    def _():
        o_ref[0] = jnp.broadcast_to(b_ref[0], o_ref.shape[1:])

    s = _silu(c_ref[...]).astype(BF16)
    o_ref[0] += _dot(s, w_ref[0].astype(BF16))


def _adaln(cc, ada_w, ada_b):
    depth, d, n3 = ada_w.shape
    tk = _pick_tile(d, (256, 128))
    return pl.pallas_call(
        _adaln_kernel,
        grid=(depth, d // tk),
        in_specs=[
            pl.BlockSpec((8, tk), lambda l, k: (0, k)),
            pl.BlockSpec((1, tk, n3), lambda l, k: (l, k, 0)),
            pl.BlockSpec((1, 1, n3), lambda l, k: (l, 0, 0)),
        ],
        out_specs=pl.BlockSpec((1, 8, n3), lambda l, k: (l, 0, 0)),
        out_shape=jax.ShapeDtypeStruct((depth, 8, n3), F32),
        compiler_params=_cparams(("parallel", "arbitrary")),
        name="adaln",
    )(cc, ada_w, ada_b.reshape(depth, 1, n3))


def _row_select(i, tm, lat_len, b, batch, mod_ref, lo, hi):
    row = i * tm + lax.broadcasted_iota(jnp.int32, (tm, 1), 0)
    vx = mod_ref[pl.ds(b, 1), lo:hi]
    vc = mod_ref[batch:batch + 1, lo:hi]
    return jnp.where(row < lat_len, vx, vc)


def _norm_modulate(x, i, tm, lat_len, b, batch, mod_ref, g_ref, d):
    y = x * lax.rsqrt(jnp.mean(x * x, axis=-1, keepdims=True) + EPS) * g_ref[...]
    shift = _row_select(i, tm, lat_len, b, batch, mod_ref, 0, d)
    scale = _row_select(i, tm, lat_len, b, batch, mod_ref, d, 2 * d)
    return (y * (1.0 + scale) + shift).astype(BF16)


def _normmod_kernel(x_ref, mod_ref, g_ref, o_ref, *, lat_len, batch, d):
    b, i = pl.program_id(0), pl.program_id(1)
    o_ref[0] = _norm_modulate(x_ref[0], i, x_ref.shape[1], lat_len, b, batch, mod_ref, g_ref, d)


def _normmod(xu, mod_l, g, lat_len):
    batch, t, d = xu.shape
    tm = _pick_tile(t, (768, 384, 256, 128))
    return pl.pallas_call(
        functools.partial(_normmod_kernel, lat_len=lat_len, batch=batch, d=d),
        grid=(batch, t // tm),
        in_specs=[
            pl.BlockSpec((1, tm, d), lambda b, i: (b, i, 0)),
            pl.BlockSpec((8, 3 * d), lambda b, i: (0, 0)),
            pl.BlockSpec((1, d), lambda b, i: (0, 0)),
        ],
        out_specs=pl.BlockSpec((1, tm, d), lambda b, i: (b, i, 0)),
        out_shape=jax.ShapeDtypeStruct((batch, t, d), BF16),
        compiler_params=_cparams(("parallel", "parallel")),
        name="normmod",
    )(xu, mod_l, g.reshape(1, d))


def _load_w_tile(w_ref, w2_ref, w_scr, shift):
    tn = w_ref.shape[1]
    rows = 256
    for r in range(0, tn - shift, rows):
        n = min(rows, tn - shift - r)
        w_scr[r:r + n, :] = w_ref[0, shift + r:shift + r + n, :].astype(BF16)
    if shift:
        w_scr[tn - shift:tn, :] = w2_ref[0].astype(BF16)


def _in_proj_kernel(h_ref, w_ref, w2_ref, o_ref, w_scr, *, shift):
    @pl.when(pl.program_id(1) == 0)
    def _():
        _load_w_tile(w_ref, w2_ref, w_scr, shift)

    o_ref[...] = _dot_nt(h_ref[...], w_scr[...]).astype(o_ref.dtype)


def _w_specs(l, c0, tn, shift, k, jmap):
    extra = shift if shift else SUBLANE
    return [
        pl.BlockSpec((1, tn, k), lambda *g: (l, c0 // tn + jmap(*g), 0)),
        pl.BlockSpec((1, extra, k), lambda *g: (l, (c0 + (jmap(*g) + 1) * tn) // extra if shift else 0, 0)),
    ]


def _in_proj(h2, wt, l, c0, ncols, tn, shift, out_dtype):
    m, k = h2.shape
    tm = _pick_tile(m, (1152, 768, 512, 384, 256, 128))
    assert c0 % tn == 0 and ncols % tn == 0 and tn % LANE == 0 and tn % max(shift, 1) == 0 and shift % 16 == 0
    return pl.pallas_call(
        functools.partial(_in_proj_kernel, shift=shift),
        grid=(ncols // tn, m // tm),
        in_specs=[pl.BlockSpec((tm, k), lambda j, i: (i, 0))] + _w_specs(l, c0, tn, shift, k, lambda j, i: j),
        out_specs=pl.BlockSpec((tm, tn), lambda j, i: (i, j)),
        out_shape=jax.ShapeDtypeStruct((m, ncols), out_dtype),
        scratch_shapes=[pltpu.VMEM((tn, k), BF16)],
        compiler_params=_cparams(("parallel", "arbitrary")),
        name="in_proj",
    )(h2, wt, wt)


def _in_proj_t_kernel(h_ref, w_ref, w2_ref, o_ref, w_scr):
    @pl.when((pl.program_id(1) == 0) & (pl.program_id(2) == 0))
    def _():
        _load_w_tile(w_ref, w2_ref, w_scr, 0)

    o_ref[0] = _dot_nt(w_scr[...], h_ref[0]).astype(o_ref.dtype)


def _in_proj_t(h, wt, l, c0, ncols, tn):
    batch, t, k = h.shape
    tm = _pick_tile(t, (768, 384, 256, 128))
    assert c0 % tn == 0 and ncols % tn == 0
    return pl.pallas_call(
        _in_proj_t_kernel,
        grid=(ncols // tn, batch, t // tm),
        in_specs=[pl.BlockSpec((1, tm, k), lambda j, b, i: (b, i, 0))]
        + _w_specs(l, c0, tn, 0, k, lambda j, b, i: j),
        out_specs=pl.BlockSpec((1, tn, tm), lambda j, b, i: (b, j, i)),
        out_shape=jax.ShapeDtypeStruct((batch, ncols, t), BF16),
        scratch_shapes=[pltpu.VMEM((tn, k), BF16)],
        compiler_params=_cparams(("parallel", "arbitrary", "arbitrary")),
        name="in_proj_t",
    )(h, wt, wt)


def _lru_kernel(x_ref, z_ref, cw_ref, cb_ref, wr_ref, br_ref, wi_ref, bi_ref, lam_ref, o_ref,
                xc_scr, hs_scr, a_scr, b_scr, h_scr, cwm_scr, *, lat_len, tc, pitch):
    batch, t, bs = x_ref.shape
    n_lat = lat_len // tc
    n_ctx = (t - lat_len) // tc

    trow = lax.broadcasted_iota(jnp.int32, (t, 1), 0)
    seg = jnp.where(trow < lat_len, trow, trow - lat_len)
    seg_len = jnp.where(trow < lat_len, lat_len, t - lat_len)
    cw = cw_ref[...]
    taps = [k for k in range(CONV_W) if k != CONV_PAD_L]
    for k in taps:
        off = k - CONV_PAD_L
        ok = (seg + off >= 0) & (seg + off < seg_len)
        cwm_scr[k] = jnp.where(ok, cw[k:k + 1], 0.0)
    cw_c = cw[CONV_PAD_L:CONV_PAD_L + 1]
    edge = 2 * SUBLANE
    for b in range(batch):
        lo, hi = SUBLANE, t - SUBLANE
        acc = cb_ref[...] + x_ref[b, lo:hi, :] * cw_c
        for k in taps:
            off = k - CONV_PAD_L
            acc = acc + x_ref[b, lo + off:hi + off, :] * cwm_scr[k, lo:hi, :]
        xc_scr[b, lo:hi, :] = acc
        for s0, keep_lo in ((0, 0), (t - edge, SUBLANE)):
            e = x_ref[b, s0:s0 + edge, :]
            acc = cb_ref[...] + e * cw_c
            for k in taps:
                acc = acc + pltpu.roll(e, (CONV_PAD_L - k) % edge, axis=0) * cwm_scr[k, s0:s0 + edge, :]
            xc_scr[b, s0 + keep_lo:s0 + keep_lo + SUBLANE, :] = acc[keep_lo:keep_lo + SUBLANE]
        hs_scr[b] = jnp.zeros((t, bs), F32)

    wr = [(0.5 * wr_ref[dr, 0]).astype(BF16) for dr in range(2)]
    wi = [(0.5 * wi_ref[dr, 0]).astype(BF16) for dr in range(2)]
    br_h = 0.5 * br_ref[...]
    bi_h = 0.5 * bi_ref[...]
    lam = lam_ref[...]
    sp = jnp.maximum(-lam, 0.0) + jnp.log1p(jnp.exp(-jnp.abs(lam)))
    la_c = (-0.5 * LRU_C) * sp

    def chunk_rows(i):
        cf = jnp.where(i < n_ctx, n_lat + i, i - n_ctx)
        cr = jnp.where(i < n_ctx, n_lat + n_ctx - 1 - i, n_lat - 1 - (i - n_ctx))
        return pl.multiple_of(cf * tc, tc), pl.multiple_of(cr * tc, tc)

    def gates(i):
        for dr, t0 in enumerate(chunk_rows(i)):
            for b in range(batch):
                xc = xc_scr[b, pl.ds(t0, tc), :]
                xb = xc.astype(BF16)
                th_r = jnp.tanh(_dot(xb, wr[dr]) + br_h[dr:dr + 1, :])
                th_i = jnp.tanh(_dot(xb, wi[dr]) + bi_h[dr:dr + 1, :])
                log_a = la_c[dr:dr + 1, :] * th_r + la_c[dr:dr + 1, :]
                a_scr[dr, pl.ds(b * pitch, tc), :] = jnp.exp(log_a)
                u = jnp.abs(jnp.tanh(log_a))
                mult = lax.rsqrt(0.5 / u + 0.5)
                b_scr[dr, pl.ds(b * pitch, tc), :] = mult * ((th_i + 1.0) * (0.5 * xc))

    def two_steps(dr, h, t_a, t_b):
        at_a = pl.ds(t_a, batch, stride=pitch)
        at_b = pl.ds(t_b, batch, stride=pitch)
        a0, b0 = a_scr[dr, at_a, :], b_scr[dr, at_a, :]
        a1, b1 = a_scr[dr, at_b, :], b_scr[dr, at_b, :]
        h_a = a0 * h + b0
        h_b = (a1 * a0) * h + (a1 * b0 + b1)
        h_scr[dr, at_a, :] = h_a
        h_scr[dr, at_b, :] = h_b
        return h_b

    def chunk(i, carry):
        hf, hr = carry
        gates(i)
        for s in range(tc // 2):
            hf = two_steps(0, hf, 2 * s, 2 * s + 1)
            hr = two_steps(1, hr, tc - 1 - 2 * s, tc - 2 - 2 * s)
        for dr, t0 in enumerate(chunk_rows(i)):
            for b in range(batch):
                hs_scr[b, pl.ds(t0, tc), :] = hs_scr[b, pl.ds(t0, tc), :] + h_scr[dr, pl.ds(b * pitch, tc), :]
        return hf, hr

    zero = jnp.zeros((batch, bs), F32)
    lax.fori_loop(0, n_lat + n_ctx, chunk, (zero, zero))

    for b in range(batch):
        o_ref[b] = (hs_scr[b] * _silu(z_ref[b])).astype(BF16)


def _lru(p, off_x, off_z, conv_w, conv_b, wr, br, wi, bi, lam, lat_len):
    batch, t, _ = p.shape
    w = conv_w.shape[1]
    bs = w // LRU_BLOCKS
    assert bs == LANE, "RG-LRU gate block must be one lane tile wide"
    tc = _pick_tile(t - lat_len, (256, 128))
    assert lat_len % tc == 0 and tc % 2 == 0
    pitch = tc + SUBLANE
    bx, bz = off_x // bs, off_z // bs
    vec = lambda rows: pl.BlockSpec((rows, bs), lambda n: (0, n))
    wspec = pl.BlockSpec((2, 1, bs, bs), lambda n: (0, n, 0, 0))
    return pl.pallas_call(
        functools.partial(_lru_kernel, lat_len=lat_len, tc=tc, pitch=pitch),
        grid=(LRU_BLOCKS,),
        in_specs=[
            pl.BlockSpec((batch, t, bs), lambda n: (0, 0, bx + n)),
            pl.BlockSpec((batch, t, bs), lambda n: (0, 0, bz + n)),
            vec(CONV_W), vec(1), wspec, vec(2), wspec, vec(2), vec(2),
        ],
        out_specs=pl.BlockSpec((batch, t, bs), lambda n: (0, 0, n)),
        out_shape=jax.ShapeDtypeStruct((batch, t, w), BF16),
        scratch_shapes=[
            pltpu.VMEM((batch, t, bs), F32),
            pltpu.VMEM((batch, t, bs), F32),
            pltpu.VMEM((2, batch * pitch, bs), F32),
            pltpu.VMEM((2, batch * pitch, bs), F32),
            pltpu.VMEM((2, batch * pitch, bs), F32),
            pltpu.VMEM((CONV_W, t, bs), F32),
        ],
        compiler_params=_cparams(("parallel",)),
        name="rglru",
    )(p, p, conv_w, conv_b.reshape(1, w), wr, br, wi, bi, lam)


def _mlstm_kernel(q_ref, kt_ref, v_ref, gc_ref, gr_ref, bc_ref, br_ref, o_ref, c_scr, m_scr, *, hd, nh):
    dr = pl.program_id(1)
    fwd = dr == 0
    sign = 1 - 2 * dr
    scale = hd ** -0.5
    log2_scale = -0.5 * math.log2(hd)

    @pl.when(pl.program_id(2) == 0)
    def _():
        c_scr[...] = jnp.zeros(c_scr.shape, F32)
        m_scr[...] = jnp.full(m_scr.shape, M_INIT, F32)

    row = lax.broadcasted_iota(jnp.int32, (ML_L, ML_L), 0)
    col = lax.broadcasted_iota(jnp.int32, (ML_L, ML_L), 1)
    keep = (col - row) * sign <= 0
    cum_c = jnp.where(keep, 1.0, 0.0).astype(BF16)
    cum_r = jnp.where((row - col) * sign <= 0, 1.0, 0.0).astype(BF16)

    g_c = gc_ref[0] + bc_ref[...]
    g_r = gr_ref[0, 0] + br_ref[...]
    x1, x2, x3 = _split3(_log_sigmoid(g_c))
    bcum_c = _dot(cum_c, x1) + _dot(cum_c, x2) + _dot(cum_c, x3)
    x1, x2, x3 = _split3(_log_sigmoid(g_r))
    bcum_r = _dot(x1, cum_r) + _dot(x2, cum_r) + _dot(x3, cum_r)

    def pick_c(arr, j):
        return jnp.where(fwd, arr[:, j:j + 1], arr[:, 2 * nh + j:2 * nh + j + 1])

    def pick_r(arr, j):
        return jnp.where(fwd, arr[j:j + 1, :], arr[2 * nh + j:2 * nh + j + 1, :])

    ones_blk = jnp.ones((ML_L, LANE), BF16)

    heads = [slice(h * hd, (h + 1) * hd) for h in range(nh)]
    qk_all = [_dot(q_ref[0, :, sl], kt_ref[0, sl, :]) for sl in heads]
    qc_all = [_dot(q_ref[0, :, sl], c_scr[h].astype(BF16)) for h, sl in enumerate(heads)]

    for h, sl in enumerate(heads):
        b_col = pick_c(bcum_c, nh + h)
        b_row = pick_r(bcum_r, nh + h)
        a_row = pick_r(g_r, h) - b_row
        m = m_scr[h:h + 1, 0:1]
        kt = kt_ref[0, sl, :]
        v_aug = jnp.concatenate([v_ref[0, :, sl], ones_blk], axis=1)

        a_vis = jnp.where(keep, a_row * LOG2_E + log2_scale, -jnp.inf)
        m2 = m * LOG2_E
        mx2 = jnp.maximum(jnp.max(a_vis, axis=1, keepdims=True) - log2_scale, m2)
        mx_b = jnp.broadcast_to(mx2, (ML_L, LANE))
        dm = jnp.exp2(a_vis - jnp.concatenate([mx_b] * (ML_L // LANE), axis=1))
        w_inter_b = jnp.exp2(m2 - mx_b)
        s = qk_all[h] * dm
        qc = qc_all[h]
        sv = _dot(s.astype(BF16), v_aug)
        num = sv[:, :hd] + jnp.concatenate([w_inter_b] * (hd // LANE), axis=1) * qc[:, :hd]
        den = (sv[:, hd:] + w_inter_b * qc[:, hd:])[:, 0:1]
        o_ref[0, 0, :, sl] = num / jnp.maximum(jnp.abs(den), jnp.exp2(-(b_col * LOG2_E + mx2)))

        b_last = jnp.where(fwd, b_row[:, ML_L - 1:ML_L], b_row[:, 0:1])
        m_in = jnp.maximum(m, jnp.max(a_row, axis=1, keepdims=True))
        decay = jnp.exp(m - m_in)
        kw_t = kt.astype(F32) * (jnp.exp(a_row - m_in) * scale)
        c_scr[h] = decay * c_scr[h] + _dot(kw_t.astype(BF16), v_aug)
        m_scr[h:h + 1, :] = jnp.broadcast_to(b_last + m_in, (1, LANE))


def _mlstm(q, kt, v, pg, gate_b, lat_len):
    batch, t, w = q.shape
    nh = ML_HEADS
    hd = w // nh
    nck = t // ML_L
    n_lat = lat_len // ML_L
    n_ctx = nck - n_lat
    ng = 4 * nh
    assert lat_len % ML_L == 0 and t % ML_L == 0 and ng <= LANE
    g_rows = jnp.swapaxes(pg[:, :, :ng].reshape(batch, nck, ML_L, ng), 2, 3)
    bias = gate_b.reshape(ng)
    bias_c = jnp.concatenate([bias, jnp.zeros((LANE - ng,), F32)]).reshape(1, LANE)
    bias_r = bias.reshape(ng, 1)

    def chunk_of(dr, i):
        fwd = jnp.where(i < n_ctx, n_lat + i, i - n_ctx)
        rev = jnp.where(i < n_ctx, nck - 1 - i, n_lat - 1 - (i - n_ctx))
        return jnp.where(dr == 0, fwd, rev)

    rows_spec = pl.BlockSpec((1, ML_L, w), lambda b, dr, i: (b, chunk_of(dr, i), 0))
    return pl.pallas_call(
        functools.partial(_mlstm_kernel, hd=hd, nh=nh),
        grid=(batch, 2, nck),
        in_specs=[
            rows_spec,
            pl.BlockSpec((1, w, ML_L), lambda b, dr, i: (b, 0, chunk_of(dr, i))),
            rows_spec,
            pl.BlockSpec((1, ML_L, LANE), lambda b, dr, i: (b, chunk_of(dr, i), 0)),
            pl.BlockSpec((1, 1, ng, ML_L), lambda b, dr, i: (b, chunk_of(dr, i), 0, 0)),
            pl.BlockSpec((1, LANE), lambda b, dr, i: (0, 0)),
            pl.BlockSpec((ng, 1), lambda b, dr, i: (0, 0)),
        ],
        out_specs=pl.BlockSpec((1, 1, ML_L, w), lambda b, dr, i: (dr, b, chunk_of(dr, i), 0)),
        out_shape=jax.ShapeDtypeStruct((2, batch, t, w), F32),
        scratch_shapes=[
            pltpu.VMEM((nh, hd, hd + LANE), F32),
            pltpu.VMEM((nh, LANE), F32),
        ],
        compiler_params=_cparams(("parallel", "parallel", "arbitrary")),
        name="mlstm",
    )(q, kt, v, pg, g_rows, bias_c, bias_r)


def _ml_out_kernel(h_ref, o_ref, z_ref, g_ref, y_ref):
    h = _sigmoid(o_ref[0]) * (h_ref[0, 0] + h_ref[1, 0])
    y = h * lax.rsqrt(jnp.mean(h * h, axis=-1, keepdims=True) + EPS) * g_ref[...]
    y_ref[0] = (y * _silu(z_ref[0])).astype(BF16)


def _ml_out(hdirs, p, off_o, off_z, ml_norm):
    _, batch, t, w = hdirs.shape
    hd = w // ML_HEADS
    tm = _pick_tile(t, (1152, 768, 384, 256, 128))
    col = lambda off: pl.BlockSpec((1, tm, hd), lambda b, i, h: (b, i, off // hd + h))
    return pl.pallas_call(
        _ml_out_kernel,
        grid=(batch, t // tm, ML_HEADS),
        in_specs=[
            pl.BlockSpec((2, 1, tm, hd), lambda b, i, h: (0, b, i, h)),
            col(off_o), col(off_z),
            pl.BlockSpec((1, hd), lambda b, i, h: (0, h)),
        ],
        out_specs=pl.BlockSpec((1, tm, hd), lambda b, i, h: (b, i, h)),
        out_shape=jax.ShapeDtypeStruct((batch, t, w), BF16),
        compiler_params=_cparams(("parallel", "parallel", "parallel")),
        name="ml_out",
    )(hdirs, p, p, ml_norm.reshape(1, w))


def _rms_rope(x, g, cos_f, sin_s):
    y = x * lax.rsqrt(jnp.mean(x * x, axis=-1, keepdims=True) + EPS) * g
    return y * cos_f + pltpu.roll(y, ATT_HD // 2, axis=1) * sin_s


def _attn_kernel(q_ref, k_ref, v_ref, z_ref, cq_ref, sq_ref, ck_ref, sk_ref, qn_ref, kn_ref, y_ref,
                 k_scr, v_scr, *, lat_len, gqa):
    qi = pl.program_id(2)
    t = k_scr.shape[0]

    @pl.when(qi == 0)
    def _():
        k_scr[...] = _rms_rope(k_ref[0], kn_ref[...], ck_ref[...], sk_ref[...]).astype(BF16)
        v_scr[:, 0:ATT_HD] = v_ref[0].astype(BF16)
        v_scr[:, ATT_HD:] = jnp.ones((t, LANE), BF16)

    def attend(rows, k0, nkeys):
        kb = k_scr[k0:k0 + nkeys, :]
        vb = v_scr[k0:k0 + nkeys, :]
        heads = [slice(g * ATT_HD, (g + 1) * ATT_HD) for g in range(gqa)]
        qb = [(_rms_rope(q_ref[0, 0:rows, sl], qn_ref[...], cq_ref[0:rows, :], sq_ref[0:rows, :])
               * (ATT_HD ** -0.5 * LOG2_E)).astype(BF16) for sl in heads]
        s = [_dot_nt(qg, kb) for qg in qb]
        for sg, sl in zip(s, heads):
            e = jnp.exp2(sg - jnp.max(sg, axis=-1, keepdims=True))
            ov = _dot(e.astype(BF16), vb)
            o = ov[:, :ATT_HD] / ov[:, ATT_HD:]
            y_ref[0, 0:rows, sl] = (o * _silu(z_ref[0, 0:rows, sl])).astype(BF16)

    @pl.when(qi < pl.num_programs(2) - 1)
    def _():
        attend(q_ref.shape[1], 0, t)

    @pl.when(qi == pl.num_programs(2) - 1)
    def _():
        attend(t - lat_len, lat_len, t - lat_len)


def _attention(p, off_q, off_k, off_v, off_z, cos_f, sin_s, q_norm, k_norm, lat_len):
    batch, t, _ = p.shape
    w = off_k - off_q
    gqa = w // ATT_HD // ATT_KV
    gw = gqa * ATT_HD
    tq = _pick_tile(lat_len, (512, 256, 128))
    assert t - lat_len <= tq and off_q % gw == 0 and off_z % gw == 0
    qz = lambda off: pl.BlockSpec((1, tq, gw), lambda b, kv, i: (b, i, off // gw + kv))
    kvs = lambda off: pl.BlockSpec((1, t, ATT_HD), lambda b, kv, i: (b, 0, off // ATT_HD + kv))
    tab_q = pl.BlockSpec((tq, ATT_HD), lambda b, kv, i: (i, 0))
    tab_k = pl.BlockSpec((t, ATT_HD), lambda b, kv, i: (0, 0))
    nrm = pl.BlockSpec((1, ATT_HD), lambda b, kv, i: (0, 0))
    return pl.pallas_call(
        functools.partial(_attn_kernel, lat_len=lat_len, gqa=gqa),
        grid=(batch, ATT_KV, lat_len // tq + 1),
        in_specs=[qz(off_q), kvs(off_k), kvs(off_v), qz(off_z), tab_q, tab_q, tab_k, tab_k, nrm, nrm],
        out_specs=pl.BlockSpec((1, tq, gw), lambda b, kv, i: (b, i, kv)),
        out_shape=jax.ShapeDtypeStruct((batch, t, w), BF16),
        scratch_shapes=[pltpu.VMEM((t, ATT_HD), BF16), pltpu.VMEM((t, ATT_HD + LANE), BF16)],
        compiler_params=_cparams(("parallel", "parallel", "arbitrary")),
        name="attention",
    )(p, p, p, p, cos_f, sin_s, cos_f, sin_s, q_norm.reshape(1, ATT_HD), k_norm.reshape(1, ATT_HD))


def _merge_kernel(y0_ref, y1_ref, y2_ref, w_ref, g0_ref, g1_ref, g2_ref, o_ref):
    acc = None
    for n, (y_ref, g_ref) in enumerate(((y0_ref, g0_ref), (y1_ref, g1_ref), (y2_ref, g2_ref))):
        term = _sigmoid(g_ref[0]) * _dot(y_ref[0], w_ref[0, n].astype(BF16))
        acc = term if acc is None else acc + term
    o_ref[0] = acc.astype(BF16)


def _merge(ys, w_br, l, p, off_g):
    batch, t, w = ys[0].shape
    d = w_br.shape[3]
    tm = _pick_tile(t, (1152, 768, 384, 256, 128))
    tn = _pick_tile(d, (256, 128))
    assert off_g % tn == 0
    yspec = pl.BlockSpec((1, tm, w), lambda b, i, j: (b, i, 0))
    gspec = lambda n: pl.BlockSpec((1, tm, tn), lambda b, i, j: (b, i, (off_g + n * d) // tn + j))
    return pl.pallas_call(
        _merge_kernel,
        grid=(batch, t // tm, d // tn),
        in_specs=[yspec, yspec, yspec,
                  pl.BlockSpec((1, N_BRANCH, w, tn), lambda b, i, j: (l, 0, 0, j)),
                  gspec(0), gspec(1), gspec(2)],
        out_specs=pl.BlockSpec((1, tm, tn), lambda b, i, j: (b, i, j)),
        out_shape=jax.ShapeDtypeStruct((batch, t, d), BF16),
        compiler_params=_cparams(("parallel", "parallel", "arbitrary")),
        name="merge",
    )(*ys, w_br, p, p, p)


def _out_kernel(s_ref, w_ref, x_ref, mod_ref, g_ref, *rest, lat_len, batch, d, with_next):
    if with_next:
        modn_ref, gn_ref, o_ref, hn_ref, w_scr = rest
    else:
        o_ref, w_scr = rest
    b, i = pl.program_id(0), pl.program_id(1)
    tm = s_ref.shape[1]

    @pl.when((b == 0) & (i == 0))
    def _():
        rows = 256
        for r in range(0, d, rows):
            w_scr[r:r + rows, :] = w_ref[0, r:r + rows, :].astype(BF16)

    y = _dot(s_ref[0], w_scr[...])
    yn = y * lax.rsqrt(jnp.mean(y * y, axis=-1, keepdims=True) + EPS) * g_ref[...]
    gate = _row_select(i, tm, lat_len, b, batch, mod_ref, 2 * d, 3 * d)
    x = x_ref[0] + gate * yn
    o_ref[0] = x
    if with_next:
        hn_ref[0] = _norm_modulate(x, i, tm, lat_len, b, batch, modn_ref, gn_ref, d)


def _out_proj(zsum, w_out, l, xu, mod_l, g, lat_len, nrows, nxt=None):
    batch, _, d = xu.shape
    tm = _pick_tile(nrows, (512, 384, 256, 128))
    rows_in = pl.BlockSpec((1, tm, d), lambda b, i: (b, i, 0))
    rows_out = rows_in
    mod_spec = pl.BlockSpec((8, 3 * d), lambda b, i: (0, 0))
    vec_spec = pl.BlockSpec((1, d), lambda b, i: (0, 0))
    with_next = nxt is not None
    extra_in = [mod_spec, vec_spec] if with_next else []
    extra_args = (nxt[0], nxt[1].reshape(1, d)) if with_next else ()
    x_shape = jax.ShapeDtypeStruct((batch, nrows, d), F32)
    return pl.pallas_call(
        functools.partial(_out_kernel, lat_len=lat_len, batch=batch, d=d, with_next=with_next),
        grid=(batch, nrows // tm),
        in_specs=[
            rows_in,
            pl.BlockSpec((1, d, d), lambda b, i: (l, 0, 0), pipeline_mode=pl.Buffered(1)),
            rows_in, mod_spec, vec_spec,
        ] + extra_in,
        out_specs=(rows_out, rows_out) if with_next else rows_out,
        out_shape=(x_shape, jax.ShapeDtypeStruct((batch, nrows, d), BF16)) if with_next else x_shape,
        scratch_shapes=[pltpu.VMEM((d, d), BF16)],
        compiler_params=_cparams(("arbitrary", "arbitrary")),
        name="out_proj",
    )(zsum, w_out, xu, mod_l, g.reshape(1, d), *extra_args)


def _rope_tables(seq, ctx_len):
    rows = seq // GRID_W
    row = jnp.repeat(jnp.arange(rows, dtype=jnp.int32), GRID_W).astype(F32)
    col = jnp.tile(jnp.arange(GRID_W, dtype=jnp.int32), rows).astype(F32)
    n_freq = ATT_HD // 4
    inv = 1.0 / (ROPE_THETA ** (jnp.arange(n_freq, dtype=F32) / n_freq))
    ang = jnp.concatenate([row[:, None] * inv, col[:, None] * inv], axis=-1)
    cos, sin = jnp.cos(ang), jnp.sin(ang)
    cos_f = jnp.concatenate([cos, cos], axis=-1)
    sin_s = jnp.concatenate([-sin, sin], axis=-1)
    cos_f = jnp.concatenate([cos_f, jnp.ones((ctx_len, ATT_HD), F32)], axis=0)
    sin_s = jnp.concatenate([sin_s, jnp.zeros((ctx_len, ATT_HD), F32)], axis=0)
    return cos_f, sin_s


def kernel(x, c, ctx, c_ctx, ada_w, ada_b, norm_pre, norm_post, w_in, lru_conv_w, lru_conv_b, lru_wr, lru_br, lru_wi, lru_bi, lru_lam, ml_gate_b, ml_norm, q_norm, k_norm, w_br, w_out):
    batch, seq, d = x.shape
    ctx_len = ctx.shape[1]
    t = ctx_len + seq
    depth = ada_w.shape[0]
    w = d
    w_kv = ATT_KV * ATT_HD
    n_gate = 4 * ML_HEADS
    assert batch + 1 <= 8

    c_lru, c_qkv, c_oz, c_gate = 0, 2 * w, 5 * w, 7 * w
    n_att = 2 * w + 2 * w_kv + N_BRANCH * d
    tn = _pick_tile(n_att, (1024, 512, 256))
    assert w % tn == 0
    off_aq, off_ak, off_av, off_az, off_mg = 0, w, w + w_kv, w + 2 * w_kv, 2 * w + 2 * w_kv

    cos_f, sin_s = _rope_tables(seq, ctx_len)
    cc = jnp.concatenate([c, c_ctx[None], jnp.zeros((8 - batch - 1, d), F32)], axis=0)
    mod = _adaln(cc, ada_w, ada_b)
    xu = jnp.concatenate([x, ctx], axis=1)
    wt_in = jnp.swapaxes(w_in, 1, 2)

    h3 = _normmod(xu, mod[0], norm_pre[0], seq)
    for l in range(depth):
        h2 = h3.reshape(batch * t, d)
        proj = lambda c0, n, tile, shift, dt: _in_proj(h2, wt_in, l, c0, n, tile, shift, dt).reshape(batch, t, n)
        p_lru = proj(c_lru, 2 * w, tn, 0, F32)
        p_q = proj(c_qkv, w, tn, 0, BF16)
        p_kt = _in_proj_t(h3, wt_in, l, c_qkv + w, w, tn)
        p_v = proj(c_qkv + 2 * w, w, tn, 0, BF16)
        p_oz = proj(c_oz, 2 * w, tn, 0, F32)
        p_gate = proj(c_gate, LANE, LANE, 0, F32)
        p_att = proj(c_gate, n_att, tn, n_gate, F32)

        y_lru = _lru(p_lru, 0, w, lru_conv_w[l], lru_conv_b[l], lru_wr[l], lru_br[l],
                     lru_wi[l], lru_bi[l], lru_lam[l], seq)
        hdirs = _mlstm(p_q, p_kt, p_v, p_gate, ml_gate_b[l], seq)
        y_ml = _ml_out(hdirs, p_oz, 0, w, ml_norm[l])
        y_att = _attention(p_att, off_aq, off_ak, off_av, off_az, cos_f, sin_s, q_norm[l], k_norm[l], seq)

        zsum = _merge((y_lru, y_ml, y_att), w_br, l, p_att, off_mg)
        if l < depth - 1:
            xu, h3 = _out_proj(zsum, w_out, l, xu, mod[l], norm_post[l], seq, t,
                               nxt=(mod[l + 1], norm_pre[l + 1]))
        else:
            xu = _out_proj(zsum, w_out, l, xu, mod[l], norm_post[l], seq, seq)

    return xu
```

```python
import functools
import math

import jax
import jax.numpy as jnp
from jax import lax
from jax.experimental import pallas as pl
from jax.experimental.pallas import tpu as pltpu

F32 = jnp.float32
BF16 = jnp.bfloat16

EPS = 1e-6
N_BRANCH = 3
LRU_BLOCKS = 16
CONV_W = 4
CONV_PAD_L = 2
LRU_C = 8.0
ML_HEADS = 8
ML_L = 256
M_INIT = -1e30
ATT_HD = 128
ATT_KV = 4
GRID_W = 64
ROPE_THETA = 10000.0
LOG2_E = 1.4426950408889634
LANE = 128
SUBLANE = 8
VMEM_LIMIT = 56 * 1024 * 1024


def _cparams(sem):
    return pltpu.CompilerParams(dimension_semantics=sem, vmem_limit_bytes=VMEM_LIMIT)


def _sigmoid(x):
    return 0.5 * jnp.tanh(0.5 * x) + 0.5


def _silu(x):
    return x * _sigmoid(x)


def _log_sigmoid(x):
    return jnp.minimum(x, 0.0) - jnp.log1p(jnp.exp(-jnp.abs(x)))


def _dot(a, b):
    return jnp.dot(a, b, preferred_element_type=F32)


def _dot_nt(a, b):
    return lax.dot_general(a, b, (((1,), (1,)), ((), ())), preferred_element_type=F32)


def _split3(x):
    x1 = x.astype(BF16)
    r1 = x - x1.astype(F32)
    x2 = r1.astype(BF16)
    x3 = (r1 - x2.astype(F32)).astype(BF16)
    return x1, x2, x3


def _pick_tile(n, cands):
    for c in cands:
        if n % c == 0:
            return c
    raise ValueError(f"no tile for {n} among {cands}")


def _adaln_kernel(c_ref, w_ref, b_ref, o_ref):
    @pl.when(pl.program_id(1) == 0)
    def _():
        o_ref[0] = jnp.broadcast_to(b_ref[0], o_ref.shape[1:])

    s = _silu(c_ref[...]).astype(BF16)
    o_ref[0] += _dot(s, w_ref[0].astype(BF16))


def _adaln(cc, ada_w, ada_b):
    depth, d, n3 = ada_w.shape
    tk = _pick_tile(d, (256, 128))
    return pl.pallas_call(
        _adaln_kernel,
        grid=(depth, d // tk),
        in_specs=[
            pl.BlockSpec((8, tk), lambda l, k: (0, k)),
            pl.BlockSpec((1, tk, n3), lambda l, k: (l, k, 0)),
            pl.BlockSpec((1, 1, n3), lambda l, k: (l, 0, 0)),
        ],
        out_specs=pl.BlockSpec((1, 8, n3), lambda l, k: (l, 0, 0)),
        out_shape=jax.ShapeDtypeStruct((depth, 8, n3), F32),
        compiler_params=_cparams(("parallel", "arbitrary")),
        name="adaln",
    )(cc, ada_w, ada_b.reshape(depth, 1, n3))


def _row_select(i, tm, lat_len, b, batch, mod_ref, lo, hi):
    row = i * tm + lax.broadcasted_iota(jnp.int32, (tm, 1), 0)
    vx = mod_ref[pl.ds(b, 1), lo:hi]
    vc = mod_ref[batch:batch + 1, lo:hi]
    return jnp.where(row < lat_len, vx, vc)


def _norm_modulate(x, i, tm, lat_len, b, batch, mod_ref, g_ref, d):
    y = x * lax.rsqrt(jnp.mean(x * x, axis=-1, keepdims=True) + EPS) * g_ref[...]
    shift = _row_select(i, tm, lat_len, b, batch, mod_ref, 0, d)
    scale = _row_select(i, tm, lat_len, b, batch, mod_ref, d, 2 * d)
    return (y * (1.0 + scale) + shift).astype(BF16)


def _normmod_kernel(x_ref, mod_ref, g_ref, o_ref, *, lat_len, batch, d):
    b, i = pl.program_id(0), pl.program_id(1)
    o_ref[0] = _norm_modulate(x_ref[0], i, x_ref.shape[1], lat_len, b, batch, mod_ref, g_ref, d)


def _normmod(xu, mod_l, g, lat_len):
    batch, t, d = xu.shape
    tm = _pick_tile(t, (768, 384, 256, 128))
    return pl.pallas_call(
        functools.partial(_normmod_kernel, lat_len=lat_len, batch=batch, d=d),
        grid=(batch, t // tm),
        in_specs=[
            pl.BlockSpec((1, tm, d), lambda b, i: (b, i, 0)),
            pl.BlockSpec((8, 3 * d), lambda b, i: (0, 0)),
            pl.BlockSpec((1, d), lambda b, i: (0, 0)),
        ],
        out_specs=pl.BlockSpec((1, tm, d), lambda b, i: (b, i, 0)),
        out_shape=jax.ShapeDtypeStruct((batch, t, d), BF16),
        compiler_params=_cparams(("parallel", "parallel")),
        name="normmod",
    )(xu, mod_l, g.reshape(1, d))


def _load_w_tile(w_ref, w2_ref, w_scr, shift):
    tn = w_ref.shape[1]
    rows = 256
    for r in range(0, tn - shift, rows):
        n = min(rows, tn - shift - r)
        w_scr[r:r + n, :] = w_ref[0, shift + r:shift + r + n, :].astype(BF16)
    if shift:
        w_scr[tn - shift:tn, :] = w2_ref[0].astype(BF16)


def _in_proj_kernel(h_ref, w_ref, w2_ref, o_ref, w_scr, *, shift):
    @pl.when(pl.program_id(1) == 0)
    def _():
        _load_w_tile(w_ref, w2_ref, w_scr, shift)

    o_ref[...] = _dot_nt(h_ref[...], w_scr[...]).astype(o_ref.dtype)


def _w_specs(l, c0, tn, shift, k, jmap):
    extra = shift if shift else SUBLANE
    return [
        pl.BlockSpec((1, tn, k), lambda *g: (l, c0 // tn + jmap(*g), 0)),
        pl.BlockSpec((1, extra, k), lambda *g: (l, (c0 + (jmap(*g) + 1) * tn) // extra if shift else 0, 0)),
    ]


def _in_proj(h2, wt, l, c0, ncols, tn, shift, out_dtype):
    m, k = h2.shape
    tm = _pick_tile(m, (1152, 768, 512, 384, 256, 128))
    assert c0 % tn == 0 and ncols % tn == 0 and tn % LANE == 0 and tn % max(shift, 1) == 0 and shift % 16 == 0
    return pl.pallas_call(
        functools.partial(_in_proj_kernel, shift=shift),
        grid=(ncols // tn, m // tm),
        in_specs=[pl.BlockSpec((tm, k), lambda j, i: (i, 0))] + _w_specs(l, c0, tn, shift, k, lambda j, i: j),
        out_specs=pl.BlockSpec((tm, tn), lambda j, i: (i, j)),
        out_shape=jax.ShapeDtypeStruct((m, ncols), out_dtype),
        scratch_shapes=[pltpu.VMEM((tn, k), BF16)],
        compiler_params=_cparams(("parallel", "arbitrary")),
        name="in_proj",
    )(h2, wt, wt)


def _in_proj_t_kernel(h_ref, w_ref, w2_ref, o_ref, w_scr):
    @pl.when((pl.program_id(1) == 0) & (pl.program_id(2) == 0))
    def _():
        _load_w_tile(w_ref, w2_ref, w_scr, 0)

    o_ref[0] = _dot_nt(w_scr[...], h_ref[0]).astype(o_ref.dtype)


def _in_proj_t(h, wt, l, c0, ncols, tn):
    batch, t, k = h.shape
    tm = _pick_tile(t, (768, 384, 256, 128))
    assert c0 % tn == 0 and ncols % tn == 0
    return pl.pallas_call(
        _in_proj_t_kernel,
        grid=(ncols // tn, batch, t // tm),
        in_specs=[pl.BlockSpec((1, tm, k), lambda j, b, i: (b, i, 0))]
        + _w_specs(l, c0, tn, 0, k, lambda j, b, i: j),
        out_specs=pl.BlockSpec((1, tn, tm), lambda j, b, i: (b, j, i)),
        out_shape=jax.ShapeDtypeStruct((batch, ncols, t), BF16),
        scratch_shapes=[pltpu.VMEM((tn, k), BF16)],
        compiler_params=_cparams(("parallel", "arbitrary", "arbitrary")),
        name="in_proj_t",
    )(h, wt, wt)


def _lru_kernel(x_ref, z_ref, cw_ref, cb_ref, wr_ref, br_ref, wi_ref, bi_ref, lam_ref, o_ref,
                xc_scr, hs_scr, a_scr, b_scr, h_scr, cwm_scr, *, lat_len, tc, pitch):
    batch, t, bs = x_ref.shape
    n_lat = lat_len // tc
    n_ctx = (t - lat_len) // tc

    trow = lax.broadcasted_iota(jnp.int32, (t, 1), 0)
    seg = jnp.where(trow < lat_len, trow, trow - lat_len)
    seg_len = jnp.where(trow < lat_len, lat_len, t - lat_len)
    cw = cw_ref[...]
    taps = [k for k in range(CONV_W) if k != CONV_PAD_L]
    for k in taps:
        off = k - CONV_PAD_L
        ok = (seg + off >= 0) & (seg + off < seg_len)
        cwm_scr[k] = jnp.where(ok, cw[k:k + 1], 0.0)
    cw_c = cw[CONV_PAD_L:CONV_PAD_L + 1]
    edge = 2 * SUBLANE
    for b in range(batch):
        lo, hi = SUBLANE, t - SUBLANE
        acc = cb_ref[...] + x_ref[b, lo:hi, :] * cw_c
        for k in taps:
            off = k - CONV_PAD_L
            acc = acc + x_ref[b, lo + off:hi + off, :] * cwm_scr[k, lo:hi, :]
        xc_scr[b, lo:hi, :] = acc
        for s0, keep_lo in ((0, 0), (t - edge, SUBLANE)):
            e = x_ref[b, s0:s0 + edge, :]
            acc = cb_ref[...] + e * cw_c
            for k in taps:
                acc = acc + pltpu.roll(e, (CONV_PAD_L - k) % edge, axis=0) * cwm_scr[k, s0:s0 + edge, :]
            xc_scr[b, s0 + keep_lo:s0 + keep_lo + SUBLANE, :] = acc[keep_lo:keep_lo + SUBLANE]
        hs_scr[b] = jnp.zeros((t, bs), F32)

    wr = [(0.5 * wr_ref[dr, 0]).astype(BF16) for dr in range(2)]
    wi = [(0.5 * wi_ref[dr, 0]).astype(BF16) for dr in range(2)]
    br_h = 0.5 * br_ref[...]
    bi_h = 0.5 * bi_ref[...]
    lam = lam_ref[...]
    sp = jnp.maximum(-lam, 0.0) + jnp.log1p(jnp.exp(-jnp.abs(lam)))
    la_c = (-0.5 * LRU_C) * sp

    def chunk_rows(i):
        cf = jnp.where(i < n_ctx, n_lat + i, i - n_ctx)
        cr = jnp.where(i < n_ctx, n_lat + n_ctx - 1 - i, n_lat - 1 - (i - n_ctx))
        return pl.multiple_of(cf * tc, tc), pl.multiple_of(cr * tc, tc)

    def gates(i):
        for dr, t0 in enumerate(chunk_rows(i)):
            for b in range(batch):
                xc = xc_scr[b, pl.ds(t0, tc), :]
                xb = xc.astype(BF16)
                th_r = jnp.tanh(_dot(xb, wr[dr]) + br_h[dr:dr + 1, :])
                th_i = jnp.tanh(_dot(xb, wi[dr]) + bi_h[dr:dr + 1, :])
                log_a = la_c[dr:dr + 1, :] * th_r + la_c[dr:dr + 1, :]
                a_scr[dr, pl.ds(b * pitch, tc), :] = jnp.exp(log_a)
                u = jnp.abs(jnp.tanh(log_a))
                mult = lax.rsqrt(0.5 / u + 0.5)
                b_scr[dr, pl.ds(b * pitch, tc), :] = mult * ((th_i + 1.0) * (0.5 * xc))

    def two_steps(dr, h, t_a, t_b):
        at_a = pl.ds(t_a, batch, stride=pitch)
        at_b = pl.ds(t_b, batch, stride=pitch)
        a0, b0 = a_scr[dr, at_a, :], b_scr[dr, at_a, :]
        a1, b1 = a_scr[dr, at_b, :], b_scr[dr, at_b, :]
        h_a = a0 * h + b0
        h_b = (a1 * a0) * h + (a1 * b0 + b1)
        h_scr[dr, at_a, :] = h_a
        h_scr[dr, at_b, :] = h_b
        return h_b

    def chunk(i, carry):
        hf, hr = carry
        gates(i)
        for s in range(tc // 2):
            hf = two_steps(0, hf, 2 * s, 2 * s + 1)
            hr = two_steps(1, hr, tc - 1 - 2 * s, tc - 2 - 2 * s)
        for dr, t0 in enumerate(chunk_rows(i)):
            for b in range(batch):
                hs_scr[b, pl.ds(t0, tc), :] = hs_scr[b, pl.ds(t0, tc), :] + h_scr[dr, pl.ds(b * pitch, tc), :]
        return hf, hr

    zero = jnp.zeros((batch, bs), F32)
    lax.fori_loop(0, n_lat + n_ctx, chunk, (zero, zero))

    for b in range(batch):
        o_ref[b] = (hs_scr[b] * _silu(z_ref[b])).astype(BF16)


def _lru(p, off_x, off_z, conv_w, conv_b, wr, br, wi, bi, lam, lat_len):
    batch, t, _ = p.shape
    w = conv_w.shape[1]
    bs = w // LRU_BLOCKS
    assert bs == LANE, "RG-LRU gate block must be one lane tile wide"
    tc = _pick_tile(t - lat_len, (256, 128))
    assert lat_len % tc == 0 and tc % 2 == 0
    pitch = tc + SUBLANE
    bx, bz = off_x // bs, off_z // bs
    vec = lambda rows: pl.BlockSpec((rows, bs), lambda n: (0, n))
    wspec = pl.BlockSpec((2, 1, bs, bs), lambda n: (0, n, 0, 0))
    return pl.pallas_call(
        functools.partial(_lru_kernel, lat_len=lat_len, tc=tc, pitch=pitch),
        grid=(LRU_BLOCKS,),
        in_specs=[
            pl.BlockSpec((batch, t, bs), lambda n: (0, 0, bx + n)),
            pl.BlockSpec((batch, t, bs), lambda n: (0, 0, bz + n)),
            vec(CONV_W), vec(1), wspec, vec(2), wspec, vec(2), vec(2),
        ],
        out_specs=pl.BlockSpec((batch, t, bs), lambda n: (0, 0, n)),
        out_shape=jax.ShapeDtypeStruct((batch, t, w), BF16),
        scratch_shapes=[
            pltpu.VMEM((batch, t, bs), F32),
            pltpu.VMEM((batch, t, bs), F32),
            pltpu.VMEM((2, batch * pitch, bs), F32),
            pltpu.VMEM((2, batch * pitch, bs), F32),
            pltpu.VMEM((2, batch * pitch, bs), F32),
            pltpu.VMEM((CONV_W, t, bs), F32),
        ],
        compiler_params=_cparams(("parallel",)),
        name="rglru",
    )(p, p, conv_w, conv_b.reshape(1, w), wr, br, wi, bi, lam)


def _mlstm_kernel(q_ref, kt_ref, v_ref, gc_ref, gr_ref, bc_ref, br_ref, og_ref, zg_ref, nrm_ref, y_ref,
                  c_scr, m_scr, hf_scr, *, hd, nh, n_lat, n_ctx):
    dr = pl.program_id(1)
    i = pl.program_id(2)
    fwd = dr == 0
    sign = 1 - 2 * dr
    scale = hd ** -0.5
    log2_scale = -0.5 * math.log2(hd)
    r0 = pl.multiple_of(_ml_chunk(dr, i, n_lat, n_ctx) * ML_L, ML_L)
    rows = pl.ds(r0, ML_L)

    @pl.when(i == 0)
    def _():
        c_scr[...] = jnp.zeros(c_scr.shape, F32)
        m_scr[...] = jnp.full(m_scr.shape, M_INIT, F32)

    @pl.when(fwd & (i == 0))
    def _():
        hf_scr[...] = jnp.zeros(hf_scr.shape, F32)

    row = lax.broadcasted_iota(jnp.int32, (ML_L, ML_L), 0)
    col = lax.broadcasted_iota(jnp.int32, (ML_L, ML_L), 1)
    keep = (col - row) * sign <= 0
    cum_c = jnp.where(keep, 1.0, 0.0).astype(BF16)
    cum_r = jnp.where((row - col) * sign <= 0, 1.0, 0.0).astype(BF16)

    g_c = gc_ref[0] + bc_ref[...]
    g_r = gr_ref[0, 0] + br_ref[...]
    x1, x2, x3 = _split3(_log_sigmoid(g_c))
    bcum_c = _dot(cum_c, x1) + _dot(cum_c, x2) + _dot(cum_c, x3)
    x1, x2, x3 = _split3(_log_sigmoid(g_r))
    bcum_r = _dot(x1, cum_r) + _dot(x2, cum_r) + _dot(x3, cum_r)

    def pick_c(arr, j):
        return jnp.where(fwd, arr[:, j:j + 1], arr[:, 2 * nh + j:2 * nh + j + 1])

    def pick_r(arr, j):
        return jnp.where(fwd, arr[j:j + 1, :], arr[2 * nh + j:2 * nh + j + 1, :])

    ones_blk = jnp.ones((ML_L, LANE), BF16)

    heads = [slice(h * hd, (h + 1) * hd) for h in range(nh)]
    qk_all = [_dot(q_ref[0, :, sl], kt_ref[0, sl, :]) for sl in heads]
    qc_all = [_dot(q_ref[0, :, sl], c_scr[h].astype(BF16)) for h, sl in enumerate(heads)]

    for h, sl in enumerate(heads):
        b_col = pick_c(bcum_c, nh + h)
        b_row = pick_r(bcum_r, nh + h)
        a_row = pick_r(g_r, h) - b_row
        m = m_scr[h:h + 1, 0:1]
        kt = kt_ref[0, sl, :]
        v_aug = jnp.concatenate([v_ref[0, :, sl], ones_blk], axis=1)

        a_vis = jnp.where(keep, a_row * LOG2_E + log2_scale, -jnp.inf)
        m2 = m * LOG2_E
        mx2 = jnp.maximum(jnp.max(a_vis, axis=1, keepdims=True) - log2_scale, m2)
        mx_b = jnp.broadcast_to(mx2, (ML_L, LANE))
        dm = jnp.exp2(a_vis - jnp.concatenate([mx_b] * (ML_L // LANE), axis=1))
        w_inter_b = jnp.exp2(m2 - mx_b)
        s = qk_all[h] * dm
        qc = qc_all[h]
        sv = _dot(s.astype(BF16), v_aug)
        num = sv[:, :hd] + jnp.concatenate([w_inter_b] * (hd // LANE), axis=1) * qc[:, :hd]
        den = (sv[:, hd:] + w_inter_b * qc[:, hd:])[:, 0:1]
        hid = num / jnp.maximum(jnp.abs(den), jnp.exp2(-(b_col * LOG2_E + mx2)))
        hf_scr[rows, sl] = hid + hf_scr[rows, sl]

        b_last = jnp.where(fwd, b_row[:, ML_L - 1:ML_L], b_row[:, 0:1])
        m_in = jnp.maximum(m, jnp.max(a_row, axis=1, keepdims=True))
        decay = jnp.exp(m - m_in)
        kw_t = kt.astype(F32) * (jnp.exp(a_row - m_in) * scale)
        c_scr[h] = decay * c_scr[h] + _dot(kw_t.astype(BF16), v_aug)
        m_scr[h:h + 1, :] = jnp.broadcast_to(b_last + m_in, (1, LANE))

    @pl.when(dr == 1)
    def _():
        for sl in heads:
            gated = _sigmoid(og_ref[0, :, sl]) * hf_scr[rows, sl]
            normed = gated * lax.rsqrt(jnp.mean(gated * gated, axis=-1, keepdims=True) + EPS) * nrm_ref[:, sl]
            y_ref[0, :, sl] = (normed * _silu(zg_ref[0, :, sl])).astype(BF16)


def _ml_chunk(dr, i, n_lat, n_ctx):
    fwd = jnp.where(i < n_ctx, n_lat + i, i - n_ctx)
    rev = jnp.where(i < n_ctx, n_lat + n_ctx - 1 - i, n_lat - 1 - (i - n_ctx))
    return jnp.where(dr == 0, fwd, rev)


def _mlstm(q, kt, v, pg, gate_b, p_oz, ml_norm, lat_len):
    batch, t, w = q.shape
    nh = ML_HEADS
    hd = w // nh
    nck = t // ML_L
    n_lat = lat_len // ML_L
    n_ctx = nck - n_lat
    ng = 4 * nh
    assert lat_len % ML_L == 0 and t % ML_L == 0 and ng <= LANE
    g_rows = jnp.swapaxes(pg[:, :, :ng].reshape(batch, nck, ML_L, ng), 2, 3)
    bias = gate_b.reshape(ng)
    bias_c = jnp.concatenate([bias, jnp.zeros((LANE - ng,), F32)]).reshape(1, LANE)
    bias_r = bias.reshape(ng, 1)

    chunk_of = lambda dr, i: _ml_chunk(dr, i, n_lat, n_ctx)
    out_chunk = lambda dr, i: chunk_of(1, jnp.where(dr == 0, 0, i))

    rows_spec = pl.BlockSpec((1, ML_L, w), lambda b, dr, i: (b, chunk_of(dr, i), 0))
    gate_spec = lambda part: pl.BlockSpec((1, ML_L, w), lambda b, dr, i: (b, out_chunk(dr, i), part))
    return pl.pallas_call(
        functools.partial(_mlstm_kernel, hd=hd, nh=nh, n_lat=n_lat, n_ctx=n_ctx),
        grid=(batch, 2, nck),
        in_specs=[
            rows_spec,
            pl.BlockSpec((1, w, ML_L), lambda b, dr, i: (b, 0, chunk_of(dr, i))),
            rows_spec,
            pl.BlockSpec((1, ML_L, LANE), lambda b, dr, i: (b, chunk_of(dr, i), 0)),
            pl.BlockSpec((1, 1, ng, ML_L), lambda b, dr, i: (b, chunk_of(dr, i), 0, 0)),
            pl.BlockSpec((1, LANE), lambda b, dr, i: (0, 0)),
            pl.BlockSpec((ng, 1), lambda b, dr, i: (0, 0)),
            gate_spec(0), gate_spec(1),
            pl.BlockSpec((1, w), lambda b, dr, i: (0, 0)),
        ],
        out_specs=pl.BlockSpec((1, ML_L, w), lambda b, dr, i: (b, out_chunk(dr, i), 0)),
        out_shape=jax.ShapeDtypeStruct((batch, t, w), BF16),
        scratch_shapes=[
            pltpu.VMEM((nh, hd, hd + LANE), F32),
            pltpu.VMEM((nh, LANE), F32),
            pltpu.VMEM((t, w), F32),
        ],
        compiler_params=_cparams(("parallel", "arbitrary", "arbitrary")),
        name="mlstm",
    )(q, kt, v, pg, g_rows, bias_c, bias_r, p_oz, p_oz, ml_norm.reshape(1, w))


def _rms_rope(x, g, cos_f, sin_s):
    y = x * lax.rsqrt(jnp.mean(x * x, axis=-1, keepdims=True) + EPS) * g
    return y * cos_f + pltpu.roll(y, ATT_HD // 2, axis=1) * sin_s


def _attn_kernel(q_ref, k_ref, v_ref, z_ref, cq_ref, sq_ref, ck_ref, sk_ref, qn_ref, kn_ref, y_ref,
                 k_scr, v_scr, *, lat_len, gqa):
    qi = pl.program_id(2)
    t = k_scr.shape[0]

    @pl.when(qi == 0)
    def _():
        k_scr[...] = _rms_rope(k_ref[0], kn_ref[...], ck_ref[...], sk_ref[...]).astype(BF16)
        v_scr[:, 0:ATT_HD] = v_ref[0].astype(BF16)
        v_scr[:, ATT_HD:] = jnp.ones((t, LANE), BF16)

    def attend(rows, k0, nkeys):
        kb = k_scr[k0:k0 + nkeys, :]
        vb = v_scr[k0:k0 + nkeys, :]
        heads = [slice(g * ATT_HD, (g + 1) * ATT_HD) for g in range(gqa)]
        qb = [(_rms_rope(q_ref[0, 0:rows, sl], qn_ref[...], cq_ref[0:rows, :], sq_ref[0:rows, :])
               * (ATT_HD ** -0.5 * LOG2_E)).astype(BF16) for sl in heads]
        s = [_dot_nt(qg, kb) for qg in qb]
        for sg, sl in zip(s, heads):
            e = jnp.exp2(sg - jnp.max(sg, axis=-1, keepdims=True))
            ov = _dot(e.astype(BF16), vb)
            o = ov[:, :ATT_HD] / ov[:, ATT_HD:]
            y_ref[0, 0:rows, sl] = (o * _silu(z_ref[0, 0:rows, sl])).astype(BF16)

    @pl.when(qi < pl.num_programs(2) - 1)
    def _():
        attend(q_ref.shape[1], 0, t)

    @pl.when(qi == pl.num_programs(2) - 1)
    def _():
        attend(t - lat_len, lat_len, t - lat_len)


def _attention(p, off_q, off_k, off_v, off_z, cos_f, sin_s, q_norm, k_norm, lat_len):
    batch, t, _ = p.shape
    w = off_k - off_q
    gqa = w // ATT_HD // ATT_KV
    gw = gqa * ATT_HD
    tq = _pick_tile(lat_len, (512, 256, 128))
    assert t - lat_len <= tq and off_q % gw == 0 and off_z % gw == 0
    qz = lambda off: pl.BlockSpec((1, tq, gw), lambda b, kv, i: (b, i, off // gw + kv))
    kvs = lambda off: pl.BlockSpec((1, t, ATT_HD), lambda b, kv, i: (b, 0, off // ATT_HD + kv))
    tab_q = pl.BlockSpec((tq, ATT_HD), lambda b, kv, i: (i, 0))
    tab_k = pl.BlockSpec((t, ATT_HD), lambda b, kv, i: (0, 0))
    nrm = pl.BlockSpec((1, ATT_HD), lambda b, kv, i: (0, 0))
    return pl.pallas_call(
        functools.partial(_attn_kernel, lat_len=lat_len, gqa=gqa),
        grid=(batch, ATT_KV, lat_len // tq + 1),
        in_specs=[qz(off_q), kvs(off_k), kvs(off_v), qz(off_z), tab_q, tab_q, tab_k, tab_k, nrm, nrm],
        out_specs=pl.BlockSpec((1, tq, gw), lambda b, kv, i: (b, i, kv)),
        out_shape=jax.ShapeDtypeStruct((batch, t, w), BF16),
        scratch_shapes=[pltpu.VMEM((t, ATT_HD), BF16), pltpu.VMEM((t, ATT_HD + LANE), BF16)],
        compiler_params=_cparams(("parallel", "parallel", "arbitrary")),
        name="attention",
    )(p, p, p, p, cos_f, sin_s, cos_f, sin_s, q_norm.reshape(1, ATT_HD), k_norm.reshape(1, ATT_HD))


def _merge_kernel(y0_ref, y1_ref, y2_ref, w_ref, g0_ref, g1_ref, g2_ref, o_ref):
    acc = None
    for n, (y_ref, g_ref) in enumerate(((y0_ref, g0_ref), (y1_ref, g1_ref), (y2_ref, g2_ref))):
        term = _sigmoid(g_ref[0]) * _dot(y_ref[0], w_ref[0, n].astype(BF16))
        acc = term if acc is None else acc + term
    o_ref[0] = acc.astype(BF16)


def _merge(ys, w_br, l, p, off_g):
    batch, t, w = ys[0].shape
    d = w_br.shape[3]
    tm = _pick_tile(t, (1152, 768, 384, 256, 128))
    tn = _pick_tile(d, (256, 128))
    assert off_g % tn == 0
    yspec = pl.BlockSpec((1, tm, w), lambda b, i, j: (b, i, 0))
    gspec = lambda n: pl.BlockSpec((1, tm, tn), lambda b, i, j: (b, i, (off_g + n * d) // tn + j))
    return pl.pallas_call(
        _merge_kernel,
        grid=(batch, t // tm, d // tn),
        in_specs=[yspec, yspec, yspec,
                  pl.BlockSpec((1, N_BRANCH, w, tn), lambda b, i, j: (l, 0, 0, j)),
                  gspec(0), gspec(1), gspec(2)],
        out_specs=pl.BlockSpec((1, tm, tn), lambda b, i, j: (b, i, j)),
        out_shape=jax.ShapeDtypeStruct((batch, t, d), BF16),
        compiler_params=_cparams(("parallel", "parallel", "arbitrary")),
        name="merge",
    )(*ys, w_br, p, p, p)


def _out_kernel(s_ref, w_ref, x_ref, mod_ref, g_ref, *rest, lat_len, batch, d, with_next):
    if with_next:
        modn_ref, gn_ref, o_ref, hn_ref, w_scr = rest
    else:
        o_ref, w_scr = rest
    b, i = pl.program_id(0), pl.program_id(1)
    tm = s_ref.shape[1]

    @pl.when((b == 0) & (i == 0))
    def _():
        rows = 256
        for r in range(0, d, rows):
            w_scr[r:r + rows, :] = w_ref[0, r:r + rows, :].astype(BF16)

    y = _dot(s_ref[0], w_scr[...])
    yn = y * lax.rsqrt(jnp.mean(y * y, axis=-1, keepdims=True) + EPS) * g_ref[...]
    gate = _row_select(i, tm, lat_len, b, batch, mod_ref, 2 * d, 3 * d)
    x = x_ref[0] + gate * yn
    o_ref[0] = x
    if with_next:
        hn_ref[0] = _norm_modulate(x, i, tm, lat_len, b, batch, modn_ref, gn_ref, d)


def _out_proj(zsum, w_out, l, xu, mod_l, g, lat_len, nrows, nxt=None):
    batch, _, d = xu.shape
    tm = _pick_tile(nrows, (512, 384, 256, 128))
    rows_in = pl.BlockSpec((1, tm, d), lambda b, i: (b, i, 0))
    rows_out = rows_in
    mod_spec = pl.BlockSpec((8, 3 * d), lambda b, i: (0, 0))
    vec_spec = pl.BlockSpec((1, d), lambda b, i: (0, 0))
    with_next = nxt is not None
    extra_in = [mod_spec, vec_spec] if with_next else []
    extra_args = (nxt[0], nxt[1].reshape(1, d)) if with_next else ()
    x_shape = jax.ShapeDtypeStruct((batch, nrows, d), F32)
    return pl.pallas_call(
        functools.partial(_out_kernel, lat_len=lat_len, batch=batch, d=d, with_next=with_next),
        grid=(batch, nrows // tm),
        in_specs=[
            rows_in,
            pl.BlockSpec((1, d, d), lambda b, i: (l, 0, 0), pipeline_mode=pl.Buffered(1)),
            rows_in, mod_spec, vec_spec,
        ] + extra_in,
        out_specs=(rows_out, rows_out) if with_next else rows_out,
        out_shape=(x_shape, jax.ShapeDtypeStruct((batch, nrows, d), BF16)) if with_next else x_shape,
        scratch_shapes=[pltpu.VMEM((d, d), BF16)],
        compiler_params=_cparams(("arbitrary", "arbitrary")),
        name="out_proj",
    )(zsum, w_out, xu, mod_l, g.reshape(1, d), *extra_args)


def _rope_tables(seq, ctx_len):
    rows = seq // GRID_W
    row = jnp.repeat(jnp.arange(rows, dtype=jnp.int32), GRID_W).astype(F32)
    col = jnp.tile(jnp.arange(GRID_W, dtype=jnp.int32), rows).astype(F32)
    n_freq = ATT_HD // 4
    inv = 1.0 / (ROPE_THETA ** (jnp.arange(n_freq, dtype=F32) / n_freq))
    ang = jnp.concatenate([row[:, None] * inv, col[:, None] * inv], axis=-1)
    cos, sin = jnp.cos(ang), jnp.sin(ang)
    cos_f = jnp.concatenate([cos, cos], axis=-1)
    sin_s = jnp.concatenate([-sin, sin], axis=-1)
    cos_f = jnp.concatenate([cos_f, jnp.ones((ctx_len, ATT_HD), F32)], axis=0)
    sin_s = jnp.concatenate([sin_s, jnp.zeros((ctx_len, ATT_HD), F32)], axis=0)
    return cos_f, sin_s


def kernel(x, c, ctx, c_ctx, ada_w, ada_b, norm_pre, norm_post, w_in, lru_conv_w, lru_conv_b, lru_wr, lru_br, lru_wi, lru_bi, lru_lam, ml_gate_b, ml_norm, q_norm, k_norm, w_br, w_out):
    batch, seq, d = x.shape
    ctx_len = ctx.shape[1]
    t = ctx_len + seq
    depth = ada_w.shape[0]
    w = d
    w_kv = ATT_KV * ATT_HD
    n_gate = 4 * ML_HEADS
    assert batch + 1 <= 8

    c_lru, c_qkv, c_oz, c_gate = 0, 2 * w, 5 * w, 7 * w
    n_att = 2 * w + 2 * w_kv + N_BRANCH * d
    tn = _pick_tile(n_att, (1024, 512, 256))
    assert w % tn == 0
    off_aq, off_ak, off_av, off_az, off_mg = 0, w, w + w_kv, w + 2 * w_kv, 2 * w + 2 * w_kv

    cos_f, sin_s = _rope_tables(seq, ctx_len)
    cc = jnp.concatenate([c, c_ctx[None], jnp.zeros((8 - batch - 1, d), F32)], axis=0)
    mod = _adaln(cc, ada_w, ada_b)
    xu = jnp.concatenate([x, ctx], axis=1)
    wt_in = jnp.swapaxes(w_in, 1, 2)

    h3 = _normmod(xu, mod[0], norm_pre[0], seq)
    for l in range(depth):
        h2 = h3.reshape(batch * t, d)
        proj = lambda c0, n, tile, shift, dt: _in_proj(h2, wt_in, l, c0, n, tile, shift, dt).reshape(batch, t, n)
        p_lru = proj(c_lru, 2 * w, tn, 0, F32)
        p_q = proj(c_qkv, w, tn, 0, BF16)
        p_kt = _in_proj_t(h3, wt_in, l, c_qkv + w, w, tn)
        p_v = proj(c_qkv + 2 * w, w, tn, 0, BF16)
        p_oz = proj(c_oz, 2 * w, tn, 0, F32)
        p_gate = proj(c_gate, LANE, LANE, 0, F32)
        p_att = proj(c_gate, n_att, tn, n_gate, F32)

        y_lru = _lru(p_lru, 0, w, lru_conv_w[l], lru_conv_b[l], lru_wr[l], lru_br[l],
                     lru_wi[l], lru_bi[l], lru_lam[l], seq)
        y_ml = _mlstm(p_q, p_kt, p_v, p_gate, ml_gate_b[l], p_oz, ml_norm[l], seq)
        y_att = _attention(p_att, off_aq, off_ak, off_av, off_az, cos_f, sin_s, q_norm[l], k_norm[l], seq)

        zsum = _merge((y_lru, y_ml, y_att), w_br, l, p_att, off_mg)
        if l < depth - 1:
            xu, h3 = _out_proj(zsum, w_out, l, xu, mod[l], norm_post[l], seq, t,
                               nxt=(mod[l + 1], norm_pre[l + 1]))
        else:
            xu = _out_proj(zsum, w_out, l, xu, mod[l], norm_post[l], seq, seq)

    return xu
```

```python
import functools
import math

import jax
import jax.numpy as jnp
from jax import lax
from jax.experimental import pallas as pl
from jax.experimental.pallas import tpu as pltpu

F32 = jnp.float32
BF16 = jnp.bfloat16

EPS = 1e-6
N_BRANCH = 3
LRU_BLOCKS = 16
CONV_W = 4
CONV_PAD_L = 2
LRU_C = 8.0
ML_HEADS = 8
ML_L = 256
M_INIT = -1e30
ATT_HD = 128
ATT_KV = 4
GRID_W = 64
ROPE_THETA = 10000.0
LOG2_E = 1.4426950408889634
LANE = 128
SUBLANE = 8
VMEM_LIMIT = 56 * 1024 * 1024


def _cparams(sem):
    return pltpu.CompilerParams(dimension_semantics=sem, vmem_limit_bytes=VMEM_LIMIT)


def _sigmoid(x):
    return 0.5 * jnp.tanh(0.5 * x) + 0.5


def _silu(x):
    return x * _sigmoid(x)


def _log_sigmoid(x):
    return jnp.minimum(x, 0.0) - jnp.log1p(jnp.exp(-jnp.abs(x)))


def _dot(a, b):
    return jnp.dot(a, b, preferred_element_type=F32)


def _dot_nt(a, b):
    return lax.dot_general(a, b, (((1,), (1,)), ((), ())), preferred_element_type=F32)


def _split3(x):
    x1 = x.astype(BF16)
    r1 = x - x1.astype(F32)
    x2 = r1.astype(BF16)
    x3 = (r1 - x2.astype(F32)).astype(BF16)
    return x1, x2, x3


def _pick_tile(n, cands):
    for c in cands:
        if n % c == 0:
            return c
    raise ValueError(f"no tile for {n} among {cands}")


def _adaln_kernel(c_ref, w_ref, b_ref, o_ref):
    @pl.when(pl.program_id(1) == 0)
    def _():
        o_ref[0] = jnp.broadcast_to(b_ref[0], o_ref.shape[1:])

    s = _silu(c_ref[...]).astype(BF16)
    o_ref[0] += _dot(s, w_ref[0].astype(BF16))


def _adaln(cc, ada_w, ada_b):
    depth, d, n3 = ada_w.shape
    tk = _pick_tile(d, (256, 128))
    return pl.pallas_call(
        _adaln_kernel,
        grid=(depth, d // tk),
        in_specs=[
            pl.BlockSpec((8, tk), lambda l, k: (0, k)),
            pl.BlockSpec((1, tk, n3), lambda l, k: (l, k, 0)),
            pl.BlockSpec((1, 1, n3), lambda l, k: (l, 0, 0)),
        ],
        out_specs=pl.BlockSpec((1, 8, n3), lambda l, k: (l, 0, 0)),
        out_shape=jax.ShapeDtypeStruct((depth, 8, n3), F32),
        compiler_params=_cparams(("parallel", "arbitrary")),
        name="adaln",
    )(cc, ada_w, ada_b.reshape(depth, 1, n3))


def _row_select(i, tm, lat_len, b, batch, mod_ref, lo, hi):
    row = i * tm + lax.broadcasted_iota(jnp.int32, (tm, 1), 0)
    vx = mod_ref[pl.ds(b, 1), lo:hi]
    vc = mod_ref[batch:batch + 1, lo:hi]
    return jnp.where(row < lat_len, vx, vc)


def _norm_modulate(x, i, tm, lat_len, b, batch, mod_ref, g_ref, d):
    y = x * lax.rsqrt(jnp.mean(x * x, axis=-1, keepdims=True) + EPS) * g_ref[...]
    shift = _row_select(i, tm, lat_len, b, batch, mod_ref, 0, d)
    scale = _row_select(i, tm, lat_len, b, batch, mod_ref, d, 2 * d)
    return (y * (1.0 + scale) + shift).astype(BF16)


def _normmod_kernel(x_ref, mod_ref, g_ref, o_ref, *, lat_len, batch, d):
    b, i = pl.program_id(0), pl.program_id(1)
    o_ref[0] = _norm_modulate(x_ref[0], i, x_ref.shape[1], lat_len, b, batch, mod_ref, g_ref, d)


def _normmod(xu, mod_l, g, lat_len):
    batch, t, d = xu.shape
    tm = _pick_tile(t, (768, 384, 256, 128))
    return pl.pallas_call(
        functools.partial(_normmod_kernel, lat_len=lat_len, batch=batch, d=d),
        grid=(batch, t // tm),
        in_specs=[
            pl.BlockSpec((1, tm, d), lambda b, i: (b, i, 0)),
            pl.BlockSpec((8, 3 * d), lambda b, i: (0, 0)),
            pl.BlockSpec((1, d), lambda b, i: (0, 0)),
        ],
        out_specs=pl.BlockSpec((1, tm, d), lambda b, i: (b, i, 0)),
        out_shape=jax.ShapeDtypeStruct((batch, t, d), BF16),
        compiler_params=_cparams(("parallel", "parallel")),
        name="normmod",
    )(xu, mod_l, g.reshape(1, d))


def _load_w_tile(w_ref, w2_ref, w_scr, shift):
    tn = w_ref.shape[1]
    rows = 256
    for r in range(0, tn - shift, rows):
        n = min(rows, tn - shift - r)
        w_scr[r:r + n, :] = w_ref[0, shift + r:shift + r + n, :].astype(BF16)
    if shift:
        w_scr[tn - shift:tn, :] = w2_ref[0].astype(BF16)


def _in_proj_kernel(h_ref, w_ref, w2_ref, o_ref, w_scr, *, shift):
    @pl.when(pl.program_id(1) == 0)
    def _():
        _load_w_tile(w_ref, w2_ref, w_scr, shift)

    o_ref[...] = _dot_nt(h_ref[...], w_scr[...]).astype(o_ref.dtype)


def _w_specs(l, c0, tn, shift, k, jmap):
    extra = shift if shift else SUBLANE
    return [
        pl.BlockSpec((1, tn, k), lambda *g: (l, c0 // tn + jmap(*g), 0)),
        pl.BlockSpec((1, extra, k), lambda *g: (l, (c0 + (jmap(*g) + 1) * tn) // extra if shift else 0, 0)),
    ]


def _in_proj(h2, wt, l, c0, ncols, tn, shift, out_dtype):
    m, k = h2.shape
    tm = _pick_tile(m, (1152, 768, 512, 384, 256, 128))
    assert c0 % tn == 0 and ncols % tn == 0 and tn % LANE == 0 and tn % max(shift, 1) == 0 and shift % 16 == 0
    return pl.pallas_call(
        functools.partial(_in_proj_kernel, shift=shift),
        grid=(ncols // tn, m // tm),
        in_specs=[pl.BlockSpec((tm, k), lambda j, i: (i, 0))] + _w_specs(l, c0, tn, shift, k, lambda j, i: j),
        out_specs=pl.BlockSpec((tm, tn), lambda j, i: (i, j)),
        out_shape=jax.ShapeDtypeStruct((m, ncols), out_dtype),
        scratch_shapes=[pltpu.VMEM((tn, k), BF16)],
        compiler_params=_cparams(("parallel", "arbitrary")),
        name="in_proj",
    )(h2, wt, wt)


def _in_proj_t_kernel(h_ref, w_ref, w2_ref, o_ref, w_scr):
    @pl.when((pl.program_id(1) == 0) & (pl.program_id(2) == 0))
    def _():
        _load_w_tile(w_ref, w2_ref, w_scr, 0)

    o_ref[0] = _dot_nt(w_scr[...], h_ref[0]).astype(o_ref.dtype)


def _in_proj_t(h, wt, l, c0, ncols, tn):
    batch, t, k = h.shape
    tm = _pick_tile(t, (768, 384, 256, 128))
    assert c0 % tn == 0 and ncols % tn == 0
    return pl.pallas_call(
        _in_proj_t_kernel,
        grid=(ncols // tn, batch, t // tm),
        in_specs=[pl.BlockSpec((1, tm, k), lambda j, b, i: (b, i, 0))]
        + _w_specs(l, c0, tn, 0, k, lambda j, b, i: j),
        out_specs=pl.BlockSpec((1, tn, tm), lambda j, b, i: (b, j, i)),
        out_shape=jax.ShapeDtypeStruct((batch, ncols, t), BF16),
        scratch_shapes=[pltpu.VMEM((tn, k), BF16)],
        compiler_params=_cparams(("parallel", "arbitrary", "arbitrary")),
        name="in_proj_t",
    )(h, wt, wt)


def _lru_kernel(x_ref, z_ref, cw_ref, cb_ref, wr_ref, br_ref, wi_ref, bi_ref, lam_ref, o_ref,
                xc_scr, hs_scr, a_scr, b_scr, h_scr, cwm_scr, *, lat_len, tc, pitch):
    batch, t, bs = x_ref.shape
    n_lat = lat_len // tc
    n_ctx = (t - lat_len) // tc

    trow = lax.broadcasted_iota(jnp.int32, (t, 1), 0)
    seg = jnp.where(trow < lat_len, trow, trow - lat_len)
    seg_len = jnp.where(trow < lat_len, lat_len, t - lat_len)
    cw = cw_ref[...]
    taps = [k for k in range(CONV_W) if k != CONV_PAD_L]
    for k in taps:
        off = k - CONV_PAD_L
        ok = (seg + off >= 0) & (seg + off < seg_len)
        cwm_scr[k] = jnp.where(ok, cw[k:k + 1], 0.0)
    cw_c = cw[CONV_PAD_L:CONV_PAD_L + 1]
    edge = 2 * SUBLANE
    for b in range(batch):
        lo, hi = SUBLANE, t - SUBLANE
        acc = cb_ref[...] + x_ref[b, lo:hi, :] * cw_c
        for k in taps:
            off = k - CONV_PAD_L
            acc = acc + x_ref[b, lo + off:hi + off, :] * cwm_scr[k, lo:hi, :]
        xc_scr[b, lo:hi, :] = acc
        for s0, keep_lo in ((0, 0), (t - edge, SUBLANE)):
            e = x_ref[b, s0:s0 + edge, :]
            acc = cb_ref[...] + e * cw_c
            for k in taps:
                acc = acc + pltpu.roll(e, (CONV_PAD_L - k) % edge, axis=0) * cwm_scr[k, s0:s0 + edge, :]
            xc_scr[b, s0 + keep_lo:s0 + keep_lo + SUBLANE, :] = acc[keep_lo:keep_lo + SUBLANE]
        hs_scr[b] = jnp.zeros((t, bs), F32)

    wr = [(0.5 * wr_ref[dr, 0]).astype(BF16) for dr in range(2)]
    wi = [(0.5 * wi_ref[dr, 0]).astype(BF16) for dr in range(2)]
    br_h = 0.5 * br_ref[...]
    bi_h = 0.5 * bi_ref[...]
    lam = lam_ref[...]
    sp = jnp.maximum(-lam, 0.0) + jnp.log1p(jnp.exp(-jnp.abs(lam)))
    la_c = (-0.5 * LRU_C) * sp

    def chunk_rows(i):
        cf = jnp.where(i < n_ctx, n_lat + i, i - n_ctx)
        cr = jnp.where(i < n_ctx, n_lat + n_ctx - 1 - i, n_lat - 1 - (i - n_ctx))
        return pl.multiple_of(cf * tc, tc), pl.multiple_of(cr * tc, tc)

    def gates(i):
        for dr, t0 in enumerate(chunk_rows(i)):
            for b in range(batch):
                xc = xc_scr[b, pl.ds(t0, tc), :]
                xb = xc.astype(BF16)
                th_r = jnp.tanh(_dot(xb, wr[dr]) + br_h[dr:dr + 1, :])
                th_i = jnp.tanh(_dot(xb, wi[dr]) + bi_h[dr:dr + 1, :])
                log_a = la_c[dr:dr + 1, :] * th_r + la_c[dr:dr + 1, :]
                a_scr[dr, pl.ds(b * pitch, tc), :] = jnp.exp(log_a)
                u = jnp.abs(jnp.tanh(log_a))
                mult = lax.rsqrt(0.5 / u + 0.5)
                b_scr[dr, pl.ds(b * pitch, tc), :] = mult * ((th_i + 1.0) * (0.5 * xc))

    def two_steps(dr, h, t_a, t_b):
        at_a = pl.ds(t_a, batch, stride=pitch)
        at_b = pl.ds(t_b, batch, stride=pitch)
        a0, b0 = a_scr[dr, at_a, :], b_scr[dr, at_a, :]
        a1, b1 = a_scr[dr, at_b, :], b_scr[dr, at_b, :]
        h_a = a0 * h + b0
        h_b = (a1 * a0) * h + (a1 * b0 + b1)
        h_scr[dr, at_a, :] = h_a
        h_scr[dr, at_b, :] = h_b
        return h_b

    def chunk(i, carry):
        hf, hr = carry
        gates(i)
        for s in range(tc // 2):
            hf = two_steps(0, hf, 2 * s, 2 * s + 1)
            hr = two_steps(1, hr, tc - 1 - 2 * s, tc - 2 - 2 * s)
        for dr, t0 in enumerate(chunk_rows(i)):
            for b in range(batch):
                hs_scr[b, pl.ds(t0, tc), :] = hs_scr[b, pl.ds(t0, tc), :] + h_scr[dr, pl.ds(b * pitch, tc), :]
        return hf, hr

    zero = jnp.zeros((batch, bs), F32)
    lax.fori_loop(0, n_lat + n_ctx, chunk, (zero, zero))

    for b in range(batch):
        o_ref[b] = (hs_scr[b] * _silu(z_ref[b])).astype(BF16)


def _lru(p, off_x, off_z, conv_w, conv_b, wr, br, wi, bi, lam, lat_len):
    batch, t, _ = p.shape
    w = conv_w.shape[1]
    bs = w // LRU_BLOCKS
    assert bs == LANE, "RG-LRU gate block must be one lane tile wide"
    tc = _pick_tile(t - lat_len, (256, 128))
    assert lat_len % tc == 0 and tc % 2 == 0
    pitch = tc + SUBLANE
    bx, bz = off_x // bs, off_z // bs
    vec = lambda rows: pl.BlockSpec((rows, bs), lambda n: (0, n))
    wspec = pl.BlockSpec((2, 1, bs, bs), lambda n: (0, n, 0, 0))
    return pl.pallas_call(
        functools.partial(_lru_kernel, lat_len=lat_len, tc=tc, pitch=pitch),
        grid=(LRU_BLOCKS,),
        in_specs=[
            pl.BlockSpec((batch, t, bs), lambda n: (0, 0, bx + n)),
            pl.BlockSpec((batch, t, bs), lambda n: (0, 0, bz + n)),
            vec(CONV_W), vec(1), wspec, vec(2), wspec, vec(2), vec(2),
        ],
        out_specs=pl.BlockSpec((batch, t, bs), lambda n: (0, 0, n)),
        out_shape=jax.ShapeDtypeStruct((batch, t, w), BF16),
        scratch_shapes=[
            pltpu.VMEM((batch, t, bs), F32),
            pltpu.VMEM((batch, t, bs), F32),
            pltpu.VMEM((2, batch * pitch, bs), F32),
            pltpu.VMEM((2, batch * pitch, bs), F32),
            pltpu.VMEM((2, batch * pitch, bs), F32),
            pltpu.VMEM((CONV_W, t, bs), F32),
        ],
        compiler_params=_cparams(("parallel",)),
        name="rglru",
    )(p, p, conv_w, conv_b.reshape(1, w), wr, br, wi, bi, lam)


def _mlstm_kernel(q_ref, kt_ref, v_ref, gc_ref, gr_ref, bc_ref, br_ref, og_ref, zg_ref, nrm_ref, y_ref,
                  c_scr, m_scr, hf_scr, *, hd, nh, n_lat, n_ctx):
    dr = pl.program_id(1)
    i = pl.program_id(2)
    fwd = dr == 0
    sign = 1 - 2 * dr
    scale = hd ** -0.5
    log2_scale = -0.5 * math.log2(hd)
    r0 = pl.multiple_of(_ml_chunk(dr, i, n_lat, n_ctx) * ML_L, ML_L)
    rows = pl.ds(r0, ML_L)

    @pl.when(i == 0)
    def _():
        c_scr[...] = jnp.zeros(c_scr.shape, F32)
        m_scr[...] = jnp.full(m_scr.shape, M_INIT, F32)

    @pl.when(fwd & (i == 0))
    def _():
        hf_scr[...] = jnp.zeros(hf_scr.shape, F32)

    row = lax.broadcasted_iota(jnp.int32, (ML_L, ML_L), 0)
    col = lax.broadcasted_iota(jnp.int32, (ML_L, ML_L), 1)
    keep = (col - row) * sign <= 0
    cum_c = jnp.where(keep, 1.0, 0.0).astype(BF16)
    cum_r = jnp.where((row - col) * sign <= 0, 1.0, 0.0).astype(BF16)

    g_c = gc_ref[0] + bc_ref[...]
    g_r = gr_ref[0, 0] + br_ref[...]
    x1, x2, x3 = _split3(_log_sigmoid(g_c))
    bcum_c = _dot(cum_c, x1) + _dot(cum_c, x2) + _dot(cum_c, x3)
    x1, x2, x3 = _split3(_log_sigmoid(g_r))
    bcum_r = _dot(x1, cum_r) + _dot(x2, cum_r) + _dot(x3, cum_r)

    def pick_c(arr, j):
        return jnp.where(fwd, arr[:, j:j + 1], arr[:, 2 * nh + j:2 * nh + j + 1])

    def pick_r(arr, j):
        return jnp.where(fwd, arr[j:j + 1, :], arr[2 * nh + j:2 * nh + j + 1, :])

    ones_blk = jnp.ones((ML_L, LANE), BF16)

    heads = [slice(h * hd, (h + 1) * hd) for h in range(nh)]
    qk_all = [_dot(q_ref[0, :, sl], kt_ref[0, sl, :]) for sl in heads]
    qc_all = [_dot(q_ref[0, :, sl], c_scr[h].astype(BF16)) for h, sl in enumerate(heads)]

    for h, sl in enumerate(heads):
        b_col = pick_c(bcum_c, nh + h)
        b_row = pick_r(bcum_r, nh + h)
        a_row = pick_r(g_r, h) - b_row
        m = m_scr[h:h + 1, 0:1]
        kt = kt_ref[0, sl, :]
        v_aug = jnp.concatenate([v_ref[0, :, sl], ones_blk], axis=1)

        a_vis = jnp.where(keep, a_row * LOG2_E + log2_scale, -jnp.inf)
        m2 = m * LOG2_E
        mx2 = jnp.maximum(jnp.max(a_vis, axis=1, keepdims=True) - log2_scale, m2)
        mx_b = jnp.broadcast_to(mx2, (ML_L, LANE))
        dm = jnp.exp2(a_vis - jnp.concatenate([mx_b] * (ML_L // LANE), axis=1))
        w_inter_b = jnp.exp2(m2 - mx_b)
        s = qk_all[h] * dm
        qc = qc_all[h]
        sv = _dot(s.astype(BF16), v_aug)
        num = sv[:, :hd] + jnp.concatenate([w_inter_b] * (hd // LANE), axis=1) * qc[:, :hd]
        den = (sv[:, hd:] + w_inter_b * qc[:, hd:])[:, 0:1]
        hid = num / jnp.maximum(jnp.abs(den), jnp.exp2(-(b_col * LOG2_E + mx2)))
        hf_scr[rows, sl] = hid + hf_scr[rows, sl]

        b_last = jnp.where(fwd, b_row[:, ML_L - 1:ML_L], b_row[:, 0:1])
        m_in = jnp.maximum(m, jnp.max(a_row, axis=1, keepdims=True))
        decay = jnp.exp(m - m_in)
        kw_t = kt.astype(F32) * (jnp.exp(a_row - m_in) * scale)
        c_scr[h] = decay * c_scr[h] + _dot(kw_t.astype(BF16), v_aug)
        m_scr[h:h + 1, :] = jnp.broadcast_to(b_last + m_in, (1, LANE))

    @pl.when(dr == 1)
    def _():
        for sl in heads:
            gated = _sigmoid(og_ref[0, :, sl]) * hf_scr[rows, sl]
            normed = gated * lax.rsqrt(jnp.mean(gated * gated, axis=-1, keepdims=True) + EPS) * nrm_ref[:, sl]
            y_ref[0, :, sl] = (normed * _silu(zg_ref[0, :, sl])).astype(BF16)


def _ml_chunk(dr, i, n_lat, n_ctx):
    fwd = jnp.where(i < n_ctx, n_lat + i, i - n_ctx)
    rev = jnp.where(i < n_ctx, n_lat + n_ctx - 1 - i, n_lat - 1 - (i - n_ctx))
    return jnp.where(dr == 0, fwd, rev)


def _mlstm(q, kt, v, pg, gate_b, p_oz, ml_norm, lat_len):
    batch, t, w = q.shape
    nh = ML_HEADS
    hd = w // nh
    nck = t // ML_L
    n_lat = lat_len // ML_L
    n_ctx = nck - n_lat
    ng = 4 * nh
    assert lat_len % ML_L == 0 and t % ML_L == 0 and ng <= LANE
    g_rows = jnp.swapaxes(pg[:, :, :ng].reshape(batch, nck, ML_L, ng), 2, 3)
    bias = gate_b.reshape(ng)
    bias_c = jnp.concatenate([bias, jnp.zeros((LANE - ng,), F32)]).reshape(1, LANE)
    bias_r = bias.reshape(ng, 1)

    chunk_of = lambda dr, i: _ml_chunk(dr, i, n_lat, n_ctx)
    out_chunk = lambda dr, i: chunk_of(1, jnp.where(dr == 0, 0, i))

    rows_spec = pl.BlockSpec((1, ML_L, w), lambda b, dr, i: (b, chunk_of(dr, i), 0))
    gate_spec = lambda part: pl.BlockSpec((1, ML_L, w), lambda b, dr, i: (b, out_chunk(dr, i), part))
    return pl.pallas_call(
        functools.partial(_mlstm_kernel, hd=hd, nh=nh, n_lat=n_lat, n_ctx=n_ctx),
        grid=(batch, 2, nck),
        in_specs=[
            rows_spec,
            pl.BlockSpec((1, w, ML_L), lambda b, dr, i: (b, 0, chunk_of(dr, i))),
            rows_spec,
            pl.BlockSpec((1, ML_L, LANE), lambda b, dr, i: (b, chunk_of(dr, i), 0)),
            pl.BlockSpec((1, 1, ng, ML_L), lambda b, dr, i: (b, chunk_of(dr, i), 0, 0)),
            pl.BlockSpec((1, LANE), lambda b, dr, i: (0, 0)),
            pl.BlockSpec((ng, 1), lambda b, dr, i: (0, 0)),
            gate_spec(0), gate_spec(1),
            pl.BlockSpec((1, w), lambda b, dr, i: (0, 0)),
        ],
        out_specs=pl.BlockSpec((1, ML_L, w), lambda b, dr, i: (b, out_chunk(dr, i), 0)),
        out_shape=jax.ShapeDtypeStruct((batch, t, w), BF16),
        scratch_shapes=[
            pltpu.VMEM((nh, hd, hd + LANE), F32),
            pltpu.VMEM((nh, LANE), F32),
            pltpu.VMEM((t, w), F32),
        ],
        compiler_params=_cparams(("parallel", "arbitrary", "arbitrary")),
        name="mlstm",
    )(q, kt, v, pg, g_rows, bias_c, bias_r, p_oz, p_oz, ml_norm.reshape(1, w))


def _rms_rope(x, g, cos_f, sin_s):
    y = x * lax.rsqrt(jnp.mean(x * x, axis=-1, keepdims=True) + EPS) * g
    return y * cos_f + pltpu.roll(y, ATT_HD // 2, axis=1) * sin_s


def _attn_kernel(q_ref, k_ref, v_ref, z_ref, cq_ref, sq_ref, ck_ref, sk_ref, qn_ref, kn_ref, y_ref,
                 k_scr, v_scr, *, lat_len, gqa):
    qi = pl.program_id(2)
    t = k_scr.shape[0]

    @pl.when(qi == 0)
    def _():
        k_scr[...] = _rms_rope(k_ref[0], kn_ref[...], ck_ref[...], sk_ref[...]).astype(BF16)
        v_scr[:, 0:ATT_HD] = v_ref[0].astype(BF16)
        v_scr[:, ATT_HD:] = jnp.ones((t, LANE), BF16)

    def attend(rows, k0, nkeys):
        kb = k_scr[k0:k0 + nkeys, :]
        vb = v_scr[k0:k0 + nkeys, :]
        heads = [slice(g * ATT_HD, (g + 1) * ATT_HD) for g in range(gqa)]
        qb = [(_rms_rope(q_ref[0, 0:rows, sl], qn_ref[...], cq_ref[0:rows, :], sq_ref[0:rows, :])
               * (ATT_HD ** -0.5 * LOG2_E)).astype(BF16) for sl in heads]
        s = [_dot_nt(qg, kb) for qg in qb]
        for sg, sl in zip(s, heads):
            e = jnp.exp2(sg - jnp.max(sg, axis=-1, keepdims=True))
            ov = _dot(e.astype(BF16), vb)
            o = ov[:, :ATT_HD] / ov[:, ATT_HD:]
            y_ref[0, 0:rows, sl] = (o * _silu(z_ref[0, 0:rows, sl])).astype(BF16)

    @pl.when(qi < pl.num_programs(2) - 1)
    def _():
        attend(q_ref.shape[1], 0, t)

    @pl.when(qi == pl.num_programs(2) - 1)
    def _():
        attend(t - lat_len, lat_len, t - lat_len)


def _attention(p, off_q, off_k, off_v, off_z, cos_f, sin_s, q_norm, k_norm, lat_len):
    batch, t, _ = p.shape
    w = off_k - off_q
    gqa = w // ATT_HD // ATT_KV
    gw = gqa * ATT_HD
    tq = _pick_tile(lat_len, (512, 256, 128))
    assert t - lat_len <= tq and off_q % gw == 0 and off_z % gw == 0
    qz = lambda off: pl.BlockSpec((1, tq, gw), lambda b, kv, i: (b, i, off // gw + kv))
    kvs = lambda off: pl.BlockSpec((1, t, ATT_HD), lambda b, kv, i: (b, 0, off // ATT_HD + kv))
    tab_q = pl.BlockSpec((tq, ATT_HD), lambda b, kv, i: (i, 0))
    tab_k = pl.BlockSpec((t, ATT_HD), lambda b, kv, i: (0, 0))
    nrm = pl.BlockSpec((1, ATT_HD), lambda b, kv, i: (0, 0))
    return pl.pallas_call(
        functools.partial(_attn_kernel, lat_len=lat_len, gqa=gqa),
        grid=(batch, ATT_KV, lat_len // tq + 1),
        in_specs=[qz(off_q), kvs(off_k), kvs(off_v), qz(off_z), tab_q, tab_q, tab_k, tab_k, nrm, nrm],
        out_specs=pl.BlockSpec((1, tq, gw), lambda b, kv, i: (b, i, kv)),
        out_shape=jax.ShapeDtypeStruct((batch, t, w), BF16),
        scratch_shapes=[pltpu.VMEM((t, ATT_HD), BF16), pltpu.VMEM((t, ATT_HD + LANE), BF16)],
        compiler_params=_cparams(("parallel", "parallel", "arbitrary")),
        name="attention",
    )(p, p, p, p, cos_f, sin_s, cos_f, sin_s, q_norm.reshape(1, ATT_HD), k_norm.reshape(1, ATT_HD))


def _cast_kernel(x_ref, o_ref):
    o_ref[...] = x_ref[...].astype(o_ref.dtype)


def _to_bf16(x):
    lead, rows, cols = x.shape[:-2], x.shape[-2], x.shape[-1]
    x3 = x.reshape(-1, rows, cols)
    tr = _pick_tile(rows, (512, 256, 128))
    out = pl.pallas_call(
        _cast_kernel,
        grid=(x3.shape[0], rows // tr),
        in_specs=[pl.BlockSpec((1, tr, cols), lambda a, r: (a, r, 0))],
        out_specs=pl.BlockSpec((1, tr, cols), lambda a, r: (a, r, 0)),
        out_shape=jax.ShapeDtypeStruct(x3.shape, BF16),
        compiler_params=_cparams(("parallel", "parallel")),
        name="cast_bf16",
    )(x3)
    return out.reshape(*lead, rows, cols)


def _merge_kernel(y0_ref, y1_ref, y2_ref, w_ref, g0_ref, g1_ref, g2_ref, o_ref):
    acc = None
    for n, (y_ref, g_ref) in enumerate(((y0_ref, g0_ref), (y1_ref, g1_ref), (y2_ref, g2_ref))):
        term = _sigmoid(g_ref[0]) * _dot(y_ref[0], w_ref[0, n])
        acc = term if acc is None else acc + term
    o_ref[0] = acc.astype(BF16)


def _merge(ys, w_br, l, p, off_g):
    batch, t, w = ys[0].shape
    d = w_br.shape[3]
    tm = _pick_tile(t, (1152, 768, 384, 256, 128))
    tn = _pick_tile(d, (256, 128))
    assert off_g % tn == 0
    yspec = pl.BlockSpec((1, tm, w), lambda b, i, j: (b, i, 0))
    gspec = lambda n: pl.BlockSpec((1, tm, tn), lambda b, i, j: (b, i, (off_g + n * d) // tn + j))
    return pl.pallas_call(
        _merge_kernel,
        grid=(batch, t // tm, d // tn),
        in_specs=[yspec, yspec, yspec,
                  pl.BlockSpec((1, N_BRANCH, w, tn), lambda b, i, j: (l, 0, 0, j)),
                  gspec(0), gspec(1), gspec(2)],
        out_specs=pl.BlockSpec((1, tm, tn), lambda b, i, j: (b, i, j)),
        out_shape=jax.ShapeDtypeStruct((batch, t, d), BF16),
        compiler_params=_cparams(("parallel", "parallel", "arbitrary")),
        name="merge",
    )(*ys, w_br, p, p, p)


def _out_kernel(s_ref, w_ref, x_ref, mod_ref, g_ref, *rest, lat_len, batch, d, with_next):
    if with_next:
        modn_ref, gn_ref, o_ref, hn_ref, w_scr = rest
    else:
        o_ref, w_scr = rest
    b, i = pl.program_id(0), pl.program_id(1)
    tm = s_ref.shape[1]

    @pl.when((b == 0) & (i == 0))
    def _():
        rows = 256
        for r in range(0, d, rows):
            w_scr[r:r + rows, :] = w_ref[0, r:r + rows, :].astype(BF16)

    y = _dot(s_ref[0], w_scr[...])
    yn = y * lax.rsqrt(jnp.mean(y * y, axis=-1, keepdims=True) + EPS) * g_ref[...]
    gate = _row_select(i, tm, lat_len, b, batch, mod_ref, 2 * d, 3 * d)
    x = x_ref[0] + gate * yn
    o_ref[0] = x
    if with_next:
        hn_ref[0] = _norm_modulate(x, i, tm, lat_len, b, batch, modn_ref, gn_ref, d)


def _out_proj(zsum, w_out, l, xu, mod_l, g, lat_len, nrows, nxt=None):
    batch, _, d = xu.shape
    tm = _pick_tile(nrows, (512, 384, 256, 128))
    rows_in = pl.BlockSpec((1, tm, d), lambda b, i: (b, i, 0))
    rows_out = rows_in
    mod_spec = pl.BlockSpec((8, 3 * d), lambda b, i: (0, 0))
    vec_spec = pl.BlockSpec((1, d), lambda b, i: (0, 0))
    with_next = nxt is not None
    extra_in = [mod_spec, vec_spec] if with_next else []
    extra_args = (nxt[0], nxt[1].reshape(1, d)) if with_next else ()
    x_shape = jax.ShapeDtypeStruct((batch, nrows, d), F32)
    return pl.pallas_call(
        functools.partial(_out_kernel, lat_len=lat_len, batch=batch, d=d, with_next=with_next),
        grid=(batch, nrows // tm),
        in_specs=[
            rows_in,
            pl.BlockSpec((1, d, d), lambda b, i: (l, 0, 0), pipeline_mode=pl.Buffered(1)),
            rows_in, mod_spec, vec_spec,
        ] + extra_in,
        out_specs=(rows_out, rows_out) if with_next else rows_out,
        out_shape=(x_shape, jax.ShapeDtypeStruct((batch, nrows, d), BF16)) if with_next else x_shape,
        scratch_shapes=[pltpu.VMEM((d, d), BF16)],
        compiler_params=_cparams(("arbitrary", "arbitrary")),
        name="out_proj",
    )(zsum, w_out, xu, mod_l, g.reshape(1, d), *extra_args)


def _rope_tables(seq, ctx_len):
    rows = seq // GRID_W
    row = jnp.repeat(jnp.arange(rows, dtype=jnp.int32), GRID_W).astype(F32)
    col = jnp.tile(jnp.arange(GRID_W, dtype=jnp.int32), rows).astype(F32)
    n_freq = ATT_HD // 4
    inv = 1.0 / (ROPE_THETA ** (jnp.arange(n_freq, dtype=F32) / n_freq))
    ang = jnp.concatenate([row[:, None] * inv, col[:, None] * inv], axis=-1)
    cos, sin = jnp.cos(ang), jnp.sin(ang)
    cos_f = jnp.concatenate([cos, cos], axis=-1)
    sin_s = jnp.concatenate([-sin, sin], axis=-1)
    cos_f = jnp.concatenate([cos_f, jnp.ones((ctx_len, ATT_HD), F32)], axis=0)
    sin_s = jnp.concatenate([sin_s, jnp.zeros((ctx_len, ATT_HD), F32)], axis=0)
    return cos_f, sin_s


def kernel(x, c, ctx, c_ctx, ada_w, ada_b, norm_pre, norm_post, w_in, lru_conv_w, lru_conv_b, lru_wr, lru_br, lru_wi, lru_bi, lru_lam, ml_gate_b, ml_norm, q_norm, k_norm, w_br, w_out):
    batch, seq, d = x.shape
    ctx_len = ctx.shape[1]
    t = ctx_len + seq
    depth = ada_w.shape[0]
    w = d
    w_kv = ATT_KV * ATT_HD
    n_gate = 4 * ML_HEADS
    assert batch + 1 <= 8

    c_lru, c_qkv, c_oz, c_gate = 0, 2 * w, 5 * w, 7 * w
    n_att = 2 * w + 2 * w_kv + N_BRANCH * d
    tn = _pick_tile(n_att, (1024, 512, 256))
    assert w % tn == 0
    off_aq, off_ak, off_av, off_az, off_mg = 0, w, w + w_kv, w + 2 * w_kv, 2 * w + 2 * w_kv

    cos_f, sin_s = _rope_tables(seq, ctx_len)
    cc = jnp.concatenate([c, c_ctx[None], jnp.zeros((8 - batch - 1, d), F32)], axis=0)
    mod = _adaln(cc, ada_w, ada_b)
    xu = jnp.concatenate([x, ctx], axis=1)
    wt_in = jnp.swapaxes(w_in, 1, 2)
    w_br = _to_bf16(w_br)

    h3 = _normmod(xu, mod[0], norm_pre[0], seq)
    for l in range(depth):
        h2 = h3.reshape(batch * t, d)
        proj = lambda c0, n, tile, shift, dt: _in_proj(h2, wt_in, l, c0, n, tile, shift, dt).reshape(batch, t, n)
        p_lru = proj(c_lru, 2 * w, tn, 0, F32)
        p_q = proj(c_qkv, w, tn, 0, BF16)
        p_kt = _in_proj_t(h3, wt_in, l, c_qkv + w, w, tn)
        p_v = proj(c_qkv + 2 * w, w, tn, 0, BF16)
        p_oz = proj(c_oz, 2 * w, tn, 0, F32)
        p_gate = proj(c_gate, LANE, LANE, 0, F32)
        p_att = proj(c_gate, n_att, tn, n_gate, F32)

        y_lru = _lru(p_lru, 0, w, lru_conv_w[l], lru_conv_b[l], lru_wr[l], lru_br[l],
                     lru_wi[l], lru_bi[l], lru_lam[l], seq)
        y_ml = _mlstm(p_q, p_kt, p_v, p_gate, ml_gate_b[l], p_oz, ml_norm[l], seq)
        y_att = _attention(p_att, off_aq, off_ak, off_av, off_az, cos_f, sin_s, q_norm[l], k_norm[l], seq)

        zsum = _merge((y_lru, y_ml, y_att), w_br, l, p_att, off_mg)
        if l < depth - 1:
            xu, h3 = _out_proj(zsum, w_out, l, xu, mod[l], norm_post[l], seq, t,
                               nxt=(mod[l + 1], norm_pre[l + 1]))
        else:
            xu = _out_proj(zsum, w_out, l, xu, mod[l], norm_post[l], seq, seq)

    return xu
```
